```python
import math
import jax, jax.numpy as jnp
from jax import lax
import numpy as np

D_MODEL = 1024
BATCH = 8
SEQ = 4096
DEPTH = 2
DEC_BATCH = 4
DEC_SEQ = 8192
PAST_LEN = 128

MIX_WIDTH = D_MODEL
ATTN_WIDTH = MIX_WIDTH // 2
HG_WIDTH = MIX_WIDTH - ATTN_WIDTH
ATTN_HEAD_DIM = 64
ATTN_HEADS = ATTN_WIDTH // (2 * ATTN_HEAD_DIM)
ROT_DIM = ATTN_HEAD_DIM // 4
ROPE_THETA = 500000.0
Q_BLOCK = 128
HG_HEAD_DIM = 128
HG_HEADS = HG_WIDTH // HG_HEAD_DIM
HG_CHUNK = 64
D_FF = ((8 * D_MODEL // 3 + 127) // 128) * 128
ALPHA = (2 * DEPTH) ** 0.25
BETA = (8 * DEPTH) ** -0.25
EPS = 1e-5
IN_WIDTH = 3 * ATTN_WIDTH + 5 * HG_WIDTH
SPLITS = [ATTN_WIDTH, 2 * ATTN_WIDTH, 3 * ATTN_WIDTH,
          3 * ATTN_WIDTH + HG_WIDTH, 3 * ATTN_WIDTH + 2 * HG_WIDTH,
          3 * ATTN_WIDTH + 3 * HG_WIDTH, 3 * ATTN_WIDTH + 4 * HG_WIDTH]

kernel_name = "hymba_diffattn_hgrn2_macaron_deepnorm_encoder"


def layer_norm(x, g, b):
    xf = x.astype(jnp.float32)
    mu = jnp.mean(xf, axis=-1, keepdims=True)
    var = jnp.mean(jnp.square(xf - mu), axis=-1, keepdims=True)
    y = (xf - mu) * lax.rsqrt(var + EPS) * g.astype(jnp.float32) + b.astype(jnp.float32)
    return y.astype(x.dtype)


def rms_norm(x, g):
    xf = x.astype(jnp.float32)
    y = xf * lax.rsqrt(jnp.mean(jnp.square(xf), axis=-1, keepdims=True) + EPS) * g.astype(jnp.float32)
    return y.astype(x.dtype)


def swiglu(x, w_gate, w_up, w_down):
    hg = jnp.einsum('bsd,df->bsf', x, w_gate)
    hu = jnp.einsum('bsd,df->bsf', x, w_up)
    return jnp.einsum('bsf,fd->bsd', jax.nn.silu(hg) * hu, w_down)


def apply_partial_rope(x, pos):
    half = ROT_DIM // 2
    inv_freq = jnp.float32(ROPE_THETA) ** (-jnp.arange(half, dtype=jnp.float32) * 2.0 / ROT_DIM)
    ang = pos[:, None] * inv_freq[None, :]
    cos = jnp.cos(ang)[None, :, None, None, :]
    sin = jnp.sin(ang)[None, :, None, None, :]
    xr = x[..., :ROT_DIM].astype(jnp.float32)
    x1, x2 = xr[..., :half], xr[..., half:]
    rot = jnp.concatenate([x1 * cos - x2 * sin, x2 * cos + x1 * sin], axis=-1).astype(x.dtype)
    return jnp.concatenate([rot, x[..., ROT_DIM:]], axis=-1)


def diff_attention(q, k, v, lam):
    B, S, H, _, Dh = q.shape
    nq = S // Q_BLOCK
    scale = 1.0 / math.sqrt(Dh)
    qb = q.reshape(B, nq, Q_BLOCK, H, 2, Dh).transpose(1, 0, 2, 3, 4, 5)

    def one_block(qblk):
        s = jnp.einsum('bqhcd,bkhcd->bhcqk', qblk, k).astype(jnp.float32) * scale
        p = jax.nn.softmax(s, axis=-1)
        p = p[:, :, 0] - lam * p[:, :, 1]
        return jnp.einsum('bhqk,bkhe->bqhe', p.astype(v.dtype), v)

    o = lax.map(one_block, qb)
    return o.transpose(1, 0, 2, 3, 4).reshape(B, S, H, v.shape[-1])


def hgrn2_chunk_scan(q, k, v, log_f):
    B, S, H, Dk = q.shape
    Dv = v.shape[-1]
    n = S // HG_CHUNK

    def to_chunks(x):
        return x.reshape(B, n, HG_CHUNK, H, x.shape[-1]).transpose(1, 0, 3, 2, 4)

    mask = jnp.tril(jnp.ones((HG_CHUNK, HG_CHUNK), dtype=bool))

    def step(state, inp):
        qc, kc, vc, gc = inp
        b = jnp.cumsum(gc, axis=2)
        rel = jnp.where(mask[:, :, None], b[:, :, :, None, :] - b[:, :, None, :, :], -jnp.inf)
        scores = jnp.einsum('bhtk,bhsk,bhtsk->bhts', qc, kc, jnp.exp(rel))
        o = (jnp.einsum('bhts,bhsv->bhtv', scores, vc)
             + jnp.einsum('bhtk,bhkv->bhtv', qc * jnp.exp(b), state))
        b_last = b[:, :, -1:, :]
        state = (state * jnp.exp(b_last)[:, :, 0, :, None]
                 + jnp.einsum('bhsk,bhsv->bhkv', kc * jnp.exp(b_last - b), vc))
        return state, o

    init = jnp.zeros((B, H, Dk, Dv), jnp.float32)
    _, o = lax.scan(step, init, (to_chunks(q), to_chunks(k), to_chunks(v), to_chunks(log_f)))
    return o.transpose(1, 0, 3, 2, 4).reshape(B, S, H, Dv)


def hgrn2_direction(q, f_logit, lb, v, reverse):
    B, S = f_logit.shape[:2]
    z = f_logit.astype(jnp.float32).reshape(B, S, HG_HEADS, HG_HEAD_DIM)
    lbh = lb.reshape(HG_HEADS, HG_HEAD_DIM)
    log_f = jnp.logaddexp(jnp.log(lbh), jnp.log1p(-lbh) + jax.nn.log_sigmoid(z))
    k = -jnp.expm1(log_f)
    if reverse:
        q, k, v, log_f = (jnp.flip(t, axis=1) for t in (q, k, v, log_f))
    o = hgrn2_chunk_scan(q, k, v, log_f)
    if reverse:
        o = jnp.flip(o, axis=1)
    return o


def token_mixer(h, w_in, w_out, lam_p, lam_init, attn_g, hg_g, lb_fwd, lb_bwd):
    B, S, _ = h.shape
    proj = jnp.einsum('bsd,de->bse', h, w_in)
    aq, ak, av, hq, hf_fwd, hf_bwd, hi, hgate = jnp.split(proj, SPLITS, axis=-1)

    pos = jnp.arange(S, dtype=jnp.float32)
    aq = apply_partial_rope(aq.reshape(B, S, ATTN_HEADS, 2, ATTN_HEAD_DIM), pos)
    ak = apply_partial_rope(ak.reshape(B, S, ATTN_HEADS, 2, ATTN_HEAD_DIM), pos)
    av = av.reshape(B, S, ATTN_HEADS, 2 * ATTN_HEAD_DIM)
    lp = lam_p.astype(jnp.float32)
    lam = jnp.exp(jnp.sum(lp[0] * lp[1])) - jnp.exp(jnp.sum(lp[2] * lp[3])) + lam_init
    ao = diff_attention(aq, ak, av, lam)
    ao = rms_norm(ao, attn_g) * (1.0 - lam_init)

    qf = jax.nn.silu(hq.astype(jnp.float32)).reshape(B, S, HG_HEADS, HG_HEAD_DIM)
    vf = hi.astype(jnp.float32).reshape(B, S, HG_HEADS, HG_HEAD_DIM)
    ho = (hgrn2_direction(qf, hf_fwd, lb_fwd, vf, False)
          + hgrn2_direction(qf, hf_bwd, lb_bwd, vf, True)).astype(h.dtype)
    ho = rms_norm(ho, hg_g) * jax.nn.silu(hgate).reshape(B, S, HG_HEADS, HG_HEAD_DIM)

    mixed = jnp.concatenate([ao.reshape(B, S, ATTN_WIDTH), ho.reshape(B, S, HG_WIDTH)], axis=-1)
    return jnp.einsum('bse,ed->bsd', mixed, w_out)


def trunk(x, w_in, w_out, attn_lambda, attn_norm_g, hg_norm_g, hg_lower_bound,
          ffn_w_gate, ffn_w_up, ffn_w_down, ln_g, ln_b):
    p = jax.nn.softmax(hg_lower_bound.astype(jnp.float32), axis=1)
    lbs = jnp.maximum(jnp.cumsum(p, axis=1) - p[:, :1], 0.0)
    for l in range(DEPTH):
        lam_init = 0.8 - 0.6 * math.exp(-0.3 * l)
        x = layer_norm(ALPHA * x + 0.5 * swiglu(x, ffn_w_gate[l, 0], ffn_w_up[l, 0], ffn_w_down[l, 0]),
                       ln_g[l, 0], ln_b[l, 0])
        x = layer_norm(ALPHA * x + token_mixer(x, w_in[l], w_out[l], attn_lambda[l], lam_init,
                                               attn_norm_g[l], hg_norm_g[l], lbs[0, l], lbs[1, l]),
                       ln_g[l, 1], ln_b[l, 1])
        x = layer_norm(ALPHA * x + 0.5 * swiglu(x, ffn_w_gate[l, 1], ffn_w_up[l, 1], ffn_w_down[l, 1]),
                       ln_g[l, 2], ln_b[l, 2])
    return x


def setup_inputs(seed: int = 0) -> dict:
    key = jax.random.key(seed)
    ks = jax.random.split(key, 14)
    f32 = jnp.float32
    x_prompt = jax.random.normal(ks[0], (BATCH, SEQ, D_MODEL), f32)
    x_sample = jax.random.normal(ks[1], (DEC_BATCH, DEC_SEQ, D_MODEL), f32)
    col_scale = jnp.concatenate([
        jnp.ones((2 * ATTN_WIDTH,), f32), jnp.full((ATTN_WIDTH,), BETA, f32),
        jnp.ones((3 * HG_WIDTH,), f32), jnp.full((HG_WIDTH,), BETA, f32),
        jnp.ones((HG_WIDTH,), f32)])
    w_in = jax.random.normal(ks[2], (DEPTH, D_MODEL, IN_WIDTH), f32) * (D_MODEL ** -0.5) * col_scale
    w_out = jax.random.normal(ks[3], (DEPTH, MIX_WIDTH, D_MODEL), f32) * (MIX_WIDTH ** -0.5) * BETA
    attn_lambda = jax.random.normal(ks[4], (DEPTH, 4, ATTN_HEAD_DIM), f32) * 0.1
    attn_norm_g = 1.0 + 0.02 * jax.random.normal(ks[5], (DEPTH, 2 * ATTN_HEAD_DIM), f32)
    hg_norm_g = 1.0 + 0.02 * jax.random.normal(ks[6], (DEPTH, HG_HEAD_DIM), f32)
    hg_lower_bound = jax.random.normal(ks[7], (2, DEPTH, HG_WIDTH), f32)
    ffn_w_gate = jax.random.normal(ks[8], (DEPTH, 2, D_MODEL, D_FF), f32) * (D_MODEL ** -0.5)
    ffn_w_up = jax.random.normal(ks[9], (DEPTH, 2, D_MODEL, D_FF), f32) * (D_MODEL ** -0.5)
    ffn_w_down = jax.random.normal(ks[10], (DEPTH, 2, D_FF, D_MODEL), f32) * (D_FF ** -0.5) * BETA
    ln_g = 1.0 + 0.02 * jax.random.normal(ks[11], (DEPTH, 3, D_MODEL), f32)
    ln_b = 0.02 * jax.random.normal(ks[12], (DEPTH, 3, D_MODEL), f32)
    return {"x_prompt": x_prompt, "x_sample": x_sample, "w_in": w_in, "w_out": w_out,
            "attn_lambda": attn_lambda, "attn_norm_g": attn_norm_g, "hg_norm_g": hg_norm_g,
            "hg_lower_bound": hg_lower_bound, "ffn_w_gate": ffn_w_gate, "ffn_w_up": ffn_w_up,
            "ffn_w_down": ffn_w_down, "ln_g": ln_g, "ln_b": ln_b}


def reference(x_prompt, x_sample, w_in, w_out, attn_lambda, attn_norm_g, hg_norm_g, hg_lower_bound,
              ffn_w_gate, ffn_w_up, ffn_w_down, ln_g, ln_b):
    y_prompt = trunk(x_prompt, w_in, w_out, attn_lambda, attn_norm_g, hg_norm_g, hg_lower_bound,
                     ffn_w_gate, ffn_w_up, ffn_w_down, ln_g, ln_b)
    y_sample = trunk(x_sample, w_in, w_out, attn_lambda, attn_norm_g, hg_norm_g, hg_lower_bound,
                     ffn_w_gate, ffn_w_up, ffn_w_down, ln_g, ln_b)
    return (y_prompt, y_sample)
```

```python
import functools
import math

import jax
import jax.numpy as jnp
from jax import lax
from jax.experimental import pallas as pl
from jax.experimental.pallas import tpu as pltpu

D_MODEL = 1024
DEPTH = 2
ATTN_WIDTH = 512
HG_WIDTH = 512
HEAD_W = 128
N_HEADS = 4
ATTN_HEAD_DIM = 64
ROT_DIM = 16
ROPE_THETA = 500000.0
D_FF = 2816
ALPHA = (2 * DEPTH) ** 0.25
EPS = 1e-5
IN_WIDTH = 3 * ATTN_WIDTH + 5 * HG_WIDTH

ROW_TILE = 512
FF_CHUNK = 512
ATTN_TQ = 256
ATTN_TK = 512
HG_BLOCK = 256
HG_STEP = 16
VMEM_LIMIT = 56 * 1024 * 1024

BF16 = jnp.bfloat16
F32 = jnp.float32


def _layer_norm(y, g, b):
    mu = jnp.mean(y, axis=-1, keepdims=True)
    yc = y - mu
    var = jnp.mean(yc * yc, axis=-1, keepdims=True)
    return yc * lax.rsqrt(var + EPS) * g + b


def _sigmoid(z):
    return 1.0 / (1.0 + jnp.exp(-z))


def _ffn_kernel(x_ref, wg_ref, wu_ref, wd_ref, g_ref, b_ref, o_ref):
    x = x_ref[...]
    xb = x.astype(BF16)
    acc = jnp.zeros((x.shape[0], D_MODEL), F32)
    for f0 in range(0, D_FF, FF_CHUNK):
        f1 = min(f0 + FF_CHUNK, D_FF)
        hg = jnp.dot(xb, wg_ref[:, f0:f1], preferred_element_type=F32)
        hu = jnp.dot(xb, wu_ref[:, f0:f1], preferred_element_type=F32)
        h = (hg * _sigmoid(hg) * hu).astype(BF16)
        acc = acc + jnp.dot(h, wd_ref[f0:f1, :], preferred_element_type=F32)
    y = ALPHA * x + 0.5 * acc
    o_ref[...] = _layer_norm(y, g_ref[...], b_ref[...])


def _ffn(x, wg, wu, wd, g, b):
    t = x.shape[0]
    row = lambda i: (i, 0)
    const = lambda i: (0, 0)
    return pl.pallas_call(
        _ffn_kernel,
        grid=(t // ROW_TILE,),
        in_specs=[
            pl.BlockSpec((ROW_TILE, D_MODEL), row),
            pl.BlockSpec((D_MODEL, D_FF), const, pipeline_mode=pl.Buffered(1)),
            pl.BlockSpec((D_MODEL, D_FF), const, pipeline_mode=pl.Buffered(1)),
            pl.BlockSpec((D_FF, D_MODEL), const, pipeline_mode=pl.Buffered(1)),
            pl.BlockSpec((1, D_MODEL), const),
            pl.BlockSpec((1, D_MODEL), const),
        ],
        out_specs=pl.BlockSpec((ROW_TILE, D_MODEL), row),
        out_shape=jax.ShapeDtypeStruct((t, D_MODEL), F32),
        compiler_params=pltpu.CompilerParams(
            dimension_semantics=("arbitrary",), vmem_limit_bytes=VMEM_LIMIT),
        name="ffn",
    )(x, wg, wu, wd, g, b)


def _rope(x, c, a, bt):
    outs = []
    for h in range(N_HEADS):
        xh = x[:, h * HEAD_W:(h + 1) * HEAD_W]
        up = pltpu.roll(xh, HEAD_W - ROT_DIM // 2, axis=1)
        dn = pltpu.roll(xh, ROT_DIM // 2, axis=1)
        outs.append(xh * c + up * a + dn * bt)
    return outs


def _log_forget(z, lb):
    return jnp.logaddexp(jnp.log(lb), jnp.log1p(-lb) + jax.nn.log_sigmoid(z))


def _in_proj_kernel(x_ref, w_ref, c_ref, a_ref, bt_ref, lbf_ref, lbb_ref,
                    aq_ref, ak_ref, av_ref, hq_ref, gf_ref, gb_ref, hi_ref, gate_ref):
    xb = x_ref[...].astype(BF16)
    c, a, bt = c_ref[...], a_ref[...], bt_ref[...]

    def proj(j):
        return jnp.dot(xb, w_ref[:, j * 512:(j + 1) * 512], preferred_element_type=F32)

    q = _rope(proj(0), c, a, bt)
    for h in range(N_HEADS):
        aq_ref[:, h * HEAD_W:(h + 1) * HEAD_W] = (q[h] * (ATTN_HEAD_DIM ** -0.5)).astype(BF16)
    k = _rope(proj(1), c, a, bt)
    for h in range(N_HEADS):
        ak_ref[:, h * HEAD_W:(h + 1) * HEAD_W] = k[h].astype(BF16)
    av_ref[...] = proj(2).astype(BF16)
    hq = proj(3)
    hq_ref[...] = hq * _sigmoid(hq)
    gf_ref[...] = _log_forget(proj(4), lbf_ref[...])
    gb_ref[...] = _log_forget(proj(5), lbb_ref[...])
    hi_ref[...] = proj(6).astype(BF16)
    gt = proj(7)
    gate_ref[...] = gt * _sigmoid(gt)


def _in_proj(x, w_in, rope_c, rope_a, rope_b, lb_f, lb_b, seq):
    t = x.shape[0]
    nseq = seq // ROW_TILE
    row = lambda i: (i, 0)
    const = lambda i: (0, 0)
    pos = lambda i: (i % nseq, 0)
    grp = pl.BlockSpec((ROW_TILE, 512), row)
    shp = lambda dt: jax.ShapeDtypeStruct((t, 512), dt)
    return pl.pallas_call(
        _in_proj_kernel,
        grid=(t // ROW_TILE,),
        in_specs=[
            pl.BlockSpec((ROW_TILE, D_MODEL), row),
            pl.BlockSpec((D_MODEL, IN_WIDTH), const, pipeline_mode=pl.Buffered(1)),
            pl.BlockSpec((ROW_TILE, HEAD_W), pos),
            pl.BlockSpec((ROW_TILE, HEAD_W), pos),
            pl.BlockSpec((ROW_TILE, HEAD_W), pos),
            pl.BlockSpec((1, HG_WIDTH), const),
            pl.BlockSpec((1, HG_WIDTH), const),
        ],
        out_specs=[grp] * 8,
        out_shape=[shp(BF16), shp(BF16), shp(BF16), shp(F32), shp(F32), shp(F32), shp(BF16), shp(F32)],
        compiler_params=pltpu.CompilerParams(
            dimension_semantics=("arbitrary",), vmem_limit_bytes=VMEM_LIMIT),
        name="in_proj",
    )(x, w_in, rope_c, rope_a, rope_b, lb_f, lb_b)


def _attn_kernel(lam_ref, q_ref, k_ref, v_ref, g_ref, o_ref, *, seq, out_scale):
    q = q_ref[...]
    lane = lax.broadcasted_iota(jnp.int32, q.shape, 1)
    zero = jnp.zeros_like(q)
    q1 = jnp.where(lane < ATTN_HEAD_DIM, q, zero)
    q2 = jnp.where(lane >= ATTN_HEAD_DIM, q, zero)
    nt = (((1,), (1,)), ((), ()))
    tq = q.shape[0]

    def body(j, carry):
        m1, l1, a1, m2, l2, a2 = carry
        k0 = pl.multiple_of(j * ATTN_TK, ATTN_TK)
        k = k_ref[pl.ds(k0, ATTN_TK), :]
        v = v_ref[pl.ds(k0, ATTN_TK), :]

        def stream(qm, m, l, a):
            s = lax.dot_general(qm, k, nt, preferred_element_type=F32)
            m_new = jnp.maximum(m, jnp.max(s, axis=1, keepdims=True))
            corr = jnp.exp(m - m_new)
            e = jnp.exp(s - m_new)
            l = corr * l + jnp.sum(e, axis=1, keepdims=True)
            a = corr * a + jnp.dot(e.astype(BF16), v, preferred_element_type=F32)
            return m_new, l, a

        m1, l1, a1 = stream(q1, m1, l1, a1)
        m2, l2, a2 = stream(q2, m2, l2, a2)
        return m1, l1, a1, m2, l2, a2

    neg = jnp.full((tq, 1), -jnp.inf, F32)
    zl = jnp.zeros((tq, 1), F32)
    za = jnp.zeros((tq, HEAD_W), F32)
    m1, l1, a1, m2, l2, a2 = lax.fori_loop(0, seq // ATTN_TK, body, (neg, zl, za, neg, zl, za))
    lam = lam_ref[0]
    o = a1 / l1 - lam * (a2 / l2)
    ms = jnp.mean(o * o, axis=-1, keepdims=True)
    o_ref[...] = (o * lax.rsqrt(ms + EPS) * g_ref[...] * out_scale).astype(BF16)


def _attn(lam, aq, ak, av, g, batch, seq, out_scale):
    t = aq.shape[0]
    nq = seq // ATTN_TQ
    return pl.pallas_call(
        functools.partial(_attn_kernel, seq=seq, out_scale=out_scale),
        grid=(batch, N_HEADS, nq),
        in_specs=[
            pl.BlockSpec(memory_space=pltpu.SMEM),
            pl.BlockSpec((ATTN_TQ, HEAD_W), lambda b, h, i: (b * nq + i, h)),
            pl.BlockSpec((seq, HEAD_W), lambda b, h, i: (b, h)),
            pl.BlockSpec((seq, HEAD_W), lambda b, h, i: (b, h)),
            pl.BlockSpec((1, HEAD_W), lambda b, h, i: (0, 0)),
        ],
        out_specs=pl.BlockSpec((ATTN_TQ, HEAD_W), lambda b, h, i: (b * nq + i, h)),
        out_shape=jax.ShapeDtypeStruct((t, ATTN_WIDTH), BF16),
        compiler_params=pltpu.CompilerParams(
            dimension_semantics=("arbitrary", "arbitrary", "arbitrary"), vmem_limit_bytes=VMEM_LIMIT),
        name="attn",
    )(lam, aq, ak, av, g)


def _cumsum16(g, reverse):
    row = lax.broadcasted_iota(jnp.int32, g.shape, 0)
    b = g
    for sh in (1, 2, 4, 8):
        if reverse:
            b = b + jnp.where(row < HG_STEP - sh, pltpu.roll(b, HG_STEP - sh, axis=0), 0.0)
        else:
            b = b + jnp.where(row >= sh, pltpu.roll(b, sh, axis=0), 0.0)
    return b


def _hgrn_step(q, g, v, st, reverse):
    kk = 1.0 - jnp.exp(g)
    b = _cumsum16(g, reverse)
    vf = v.astype(F32)
    row = lax.broadcasted_iota(jnp.int32, g.shape, 0)
    qd = (q * jnp.exp(b)).astype(BF16)
    o = lax.dot_general(qd, st.astype(BF16), (((1,), (1,)), ((), ())), preferred_element_type=F32)
    for s in range(HG_STEP):
        mask = (row <= s) if reverse else (row >= s)
        d = jnp.where(mask, b - b[s:s + 1, :], -jnp.inf)
        w = q * jnp.exp(d) * kk[s:s + 1, :]
        o = o + jnp.sum(w, axis=1, keepdims=True) * vf[s:s + 1, :]
    b_end = b[0:1, :] if reverse else b[HG_STEP - 1:HG_STEP, :]
    kd = (kk * jnp.exp(b_end - b)).astype(BF16)
    upd = lax.dot_general(v, kd, (((0,), (0,)), ((), ())), preferred_element_type=F32)
    return o, st * jnp.exp(b_end) + upd


def _hgrn_kernel(qf_ref, gf_ref, vf_ref, qb_ref, gb_ref, vb_ref, of_ref, ob_ref, sf_ref, sb_ref):
    @pl.when(pl.program_id(1) == 0)
    def _():
        sf_ref[...] = jnp.zeros_like(sf_ref)
        sb_ref[...] = jnp.zeros_like(sb_ref)

    nstep = HG_BLOCK // HG_STEP

    def body(i, carry):
        rf = pl.multiple_of(i * HG_STEP, HG_STEP)
        rb = pl.multiple_of((nstep - 1 - i) * HG_STEP, HG_STEP)
        for h in range(N_HEADS):
            cols = slice(h * HEAD_W, (h + 1) * HEAD_W)
            o, st = _hgrn_step(qf_ref[pl.ds(rf, HG_STEP), cols], gf_ref[pl.ds(rf, HG_STEP), cols],
                               vf_ref[pl.ds(rf, HG_STEP), cols], sf_ref[h], False)
            of_ref[pl.ds(rf, HG_STEP), cols] = o
            sf_ref[h] = st
            o, st = _hgrn_step(qb_ref[pl.ds(rb, HG_STEP), cols], gb_ref[pl.ds(rb, HG_STEP), cols],
                               vb_ref[pl.ds(rb, HG_STEP), cols], sb_ref[h], True)
            ob_ref[pl.ds(rb, HG_STEP), cols] = o
            sb_ref[h] = st
        return carry

    lax.fori_loop(0, nstep, body, 0)


def _hgrn(hq, gf, gb, hi, batch, seq):
    t = hq.shape[0]
    nb = seq // HG_BLOCK
    fwd = pl.BlockSpec((HG_BLOCK, HG_WIDTH), lambda b, j: (b * nb + j, 0))
    bwd = pl.BlockSpec((HG_BLOCK, HG_WIDTH), lambda b, j: (b * nb + nb - 1 - j, 0))
    return pl.pallas_call(
        _hgrn_kernel,
        grid=(batch, nb),
        in_specs=[fwd, fwd, fwd, bwd, bwd, bwd],
        out_specs=[fwd, bwd],
        out_shape=[jax.ShapeDtypeStruct((t, HG_WIDTH), F32)] * 2,
        scratch_shapes=[pltpu.VMEM((N_HEADS, HEAD_W, HEAD_W), F32)] * 2,
        compiler_params=pltpu.CompilerParams(
            dimension_semantics=("arbitrary", "arbitrary"), vmem_limit_bytes=VMEM_LIMIT),
        name="hgrn",
    )(hq, gf, hi, hq, gb, hi)


def _out_proj_kernel(x_ref, ao_ref, of_ref, ob_ref, gate_ref, hgg_ref, w_ref, g_ref, b_ref, o_ref):
    x = x_ref[...]
    ho = of_ref[...] + ob_ref[...]
    gate = gate_ref[...]
    hgg = hgg_ref[...]
    mix = jnp.dot(ao_ref[...], w_ref[0:ATTN_WIDTH, :], preferred_element_type=F32)
    for h in range(N_HEADS):
        cols = slice(h * HEAD_W, (h + 1) * HEAD_W)
        oh = ho[:, cols]
        ms = jnp.mean(oh * oh, axis=-1, keepdims=True)
        nh = (oh * lax.rsqrt(ms + EPS) * hgg * gate[:, cols]).astype(BF16)
        mix = mix + jnp.dot(nh, w_ref[ATTN_WIDTH + h * HEAD_W:ATTN_WIDTH + (h + 1) * HEAD_W, :],
                            preferred_element_type=F32)
    y = ALPHA * x + mix
    o_ref[...] = _layer_norm(y, g_ref[...], b_ref[...])


def _out_proj(x, ao, of, ob, gate, hgg, w_out, g, b):
    t = x.shape[0]
    row = lambda i: (i, 0)
    const = lambda i: (0, 0)
    grp = pl.BlockSpec((ROW_TILE, 512), row)
    return pl.pallas_call(
        _out_proj_kernel,
        grid=(t // ROW_TILE,),
        in_specs=[
            pl.BlockSpec((ROW_TILE, D_MODEL), row),
            grp, grp, grp, grp,
            pl.BlockSpec((1, HEAD_W), const),
            pl.BlockSpec((D_MODEL, D_MODEL), const, pipeline_mode=pl.Buffered(1)),
            pl.BlockSpec((1, D_MODEL), const),
            pl.BlockSpec((1, D_MODEL), const),
        ],
        out_specs=pl.BlockSpec((ROW_TILE, D_MODEL), row),
        out_shape=jax.ShapeDtypeStruct((t, D_MODEL), F32),
        compiler_params=pltpu.CompilerParams(
            dimension_semantics=("arbitrary",), vmem_limit_bytes=VMEM_LIMIT),
        name="out_proj",
    )(x, ao, of, ob, gate, hgg, w_out, g, b)


def _rope_tables(seq):
    half = ROT_DIM // 2
    inv_freq = jnp.float32(ROPE_THETA) ** (-jnp.arange(half, dtype=F32) * 2.0 / ROT_DIM)
    ang = jnp.arange(seq, dtype=F32)[:, None] * inv_freq[None, :]
    cos, sin = jnp.cos(ang), jnp.sin(ang)
    ones = jnp.ones((seq, ATTN_HEAD_DIM - ROT_DIM), F32)
    zeros = jnp.zeros((seq, ATTN_HEAD_DIM - ROT_DIM), F32)
    z8 = jnp.zeros((seq, half), F32)
    comp_c = jnp.concatenate([cos, cos, ones], axis=1)
    comp_a = jnp.concatenate([-sin, z8, zeros], axis=1)
    comp_b = jnp.concatenate([z8, sin, zeros], axis=1)
    two = lambda m: jnp.concatenate([m, m], axis=1)
    return two(comp_c), two(comp_a), two(comp_b)


def _trunk(x3, p):
    batch, seq, _ = x3.shape
    x = x3.reshape(batch * seq, D_MODEL)
    rope_c, rope_a, rope_b = _rope_tables(seq)
    for l in range(DEPTH):
        lam_init = 0.8 - 0.6 * math.exp(-0.3 * l)
        x = _ffn(x, p["wg"][l][0], p["wu"][l][0], p["wd"][l][0], p["ln_g"][l][0], p["ln_b"][l][0])
        aq, ak, av, hq, gf, gb, hi, gate = _in_proj(
            x, p["w_in"][l], rope_c, rope_a, rope_b, p["lbs"][0][l], p["lbs"][1][l], seq)
        ao = _attn(p["lam"][l], aq, ak, av, p["attn_g"][l], batch, seq, 1.0 - lam_init)
        of, ob = _hgrn(hq, gf, gb, hi, batch, seq)
        x = _out_proj(x, ao, of, ob, gate, p["hg_g"][l], p["w_out"][l], p["ln_g"][l][1], p["ln_b"][l][1])
        x = _ffn(x, p["wg"][l][1], p["wu"][l][1], p["wd"][l][1], p["ln_g"][l][2], p["ln_b"][l][2])
    return x.reshape(batch, seq, D_MODEL)


def kernel(x_prompt, x_sample, w_in, w_out, attn_lambda, attn_norm_g, hg_norm_g, hg_lower_bound,
           ffn_w_gate, ffn_w_up, ffn_w_down, ln_g, ln_b):
    sm = jax.nn.softmax(hg_lower_bound.astype(F32), axis=1)
    lbs = jnp.maximum(jnp.cumsum(sm, axis=1) - sm[:, :1], 0.0)
    lp = attn_lambda.astype(F32)
    lam_init = jnp.asarray([0.8 - 0.6 * math.exp(-0.3 * l) for l in range(DEPTH)], F32)
    lam = (jnp.exp(jnp.sum(lp[:, 0] * lp[:, 1], axis=-1))
           - jnp.exp(jnp.sum(lp[:, 2] * lp[:, 3], axis=-1)) + lam_init)
    p = {
        "w_in": w_in.astype(BF16), "w_out": w_out.astype(BF16),
        "wg": ffn_w_gate.astype(BF16), "wu": ffn_w_up.astype(BF16), "wd": ffn_w_down.astype(BF16),
        "ln_g": ln_g.reshape(DEPTH, 3, 1, D_MODEL), "ln_b": ln_b.reshape(DEPTH, 3, 1, D_MODEL),
        "attn_g": attn_norm_g.reshape(DEPTH, 1, HEAD_W), "hg_g": hg_norm_g.reshape(DEPTH, 1, HEAD_W),
        "lbs": lbs.reshape(2, DEPTH, 1, HG_WIDTH), "lam": lam.reshape(DEPTH, 1),
    }
    return (_trunk(x_prompt, p), _trunk(x_sample, p))
```

```python
import functools
import math

import jax
import jax.numpy as jnp
from jax import lax
from jax.experimental import pallas as pl
from jax.experimental.pallas import tpu as pltpu

D_MODEL = 1024
DEPTH = 2
ATTN_WIDTH = 512
HG_WIDTH = 512
HEAD_W = 128
N_HEADS = 4
ATTN_HEAD_DIM = 64
ROT_DIM = 16
ROPE_THETA = 500000.0
D_FF = 2816
ALPHA = (2 * DEPTH) ** 0.25
EPS = 1e-5
IN_WIDTH = 3 * ATTN_WIDTH + 5 * HG_WIDTH
ATTN_Q_SCALE = ATTN_HEAD_DIM ** -0.5 * math.log2(math.e)

ROW_TILE = 512
FF_CHUNK = 512
ATTN_TQ = 256
ATTN_TK = 512
HG_BLOCK = 256
HG_STEP = 16
VMEM_LIMIT = 56 * 1024 * 1024

BF16 = jnp.bfloat16
F32 = jnp.float32


def _layer_norm(y, g, b):
    mu = jnp.mean(y, axis=-1, keepdims=True)
    yc = y - mu
    var = jnp.mean(yc * yc, axis=-1, keepdims=True)
    return yc * lax.rsqrt(var + EPS) * g + b


def _sigmoid(z):
    return 1.0 / (1.0 + jnp.exp(-z))


def _ffn_kernel(x_ref, wg_ref, wu_ref, wd_ref, g_ref, b_ref, o_ref):
    x = x_ref[...]
    xb = x.astype(BF16)
    acc = jnp.zeros((x.shape[0], D_MODEL), F32)
    for f0 in range(0, D_FF, FF_CHUNK):
        f1 = min(f0 + FF_CHUNK, D_FF)
        hg = jnp.dot(xb, wg_ref[:, f0:f1], preferred_element_type=F32)
        hu = jnp.dot(xb, wu_ref[:, f0:f1], preferred_element_type=F32)
        h = (hg * _sigmoid(hg) * hu).astype(BF16)
        acc = acc + jnp.dot(h, wd_ref[f0:f1, :], preferred_element_type=F32)
    y = ALPHA * x + 0.5 * acc
    o_ref[...] = _layer_norm(y, g_ref[...], b_ref[...])


def _ffn(x, wg, wu, wd, g, b):
    t = x.shape[0]
    row = lambda i: (i, 0)
    const = lambda i: (0, 0)
    return pl.pallas_call(
        _ffn_kernel,
        grid=(t // ROW_TILE,),
        in_specs=[
            pl.BlockSpec((ROW_TILE, D_MODEL), row),
            pl.BlockSpec((D_MODEL, D_FF), const, pipeline_mode=pl.Buffered(1)),
            pl.BlockSpec((D_MODEL, D_FF), const, pipeline_mode=pl.Buffered(1)),
            pl.BlockSpec((D_FF, D_MODEL), const, pipeline_mode=pl.Buffered(1)),
            pl.BlockSpec((1, D_MODEL), const),
            pl.BlockSpec((1, D_MODEL), const),
        ],
        out_specs=pl.BlockSpec((ROW_TILE, D_MODEL), row),
        out_shape=jax.ShapeDtypeStruct((t, D_MODEL), F32),
        compiler_params=pltpu.CompilerParams(
            dimension_semantics=("arbitrary",), vmem_limit_bytes=VMEM_LIMIT),
        name="ffn",
    )(x, wg, wu, wd, g, b)


def _rope(x, c, a, bt):
    outs = []
    for h in range(N_HEADS):
        xh = x[:, h * HEAD_W:(h + 1) * HEAD_W]
        up = pltpu.roll(xh, HEAD_W - ROT_DIM // 2, axis=1)
        dn = pltpu.roll(xh, ROT_DIM // 2, axis=1)
        outs.append(xh * c + up * a + dn * bt)
    return outs


def _log_forget(z, lb):
    return jnp.logaddexp(jnp.log(lb), jnp.log1p(-lb) + jax.nn.log_sigmoid(z))


def _in_proj_kernel(x_ref, w_ref, c_ref, a_ref, bt_ref, lbf_ref, lbb_ref,
                    aq_ref, ak_ref, av_ref, hq_ref, gf_ref, gb_ref, hi_ref, gate_ref):
    xb = x_ref[...].astype(BF16)
    c, a, bt = c_ref[...], a_ref[...], bt_ref[...]

    def proj(j):
        return jnp.dot(xb, w_ref[:, j * 512:(j + 1) * 512], preferred_element_type=F32)

    q = _rope(proj(0), c, a, bt)
    for h in range(N_HEADS):
        aq_ref[:, h * HEAD_W:(h + 1) * HEAD_W] = (q[h] * ATTN_Q_SCALE).astype(BF16)
    k = _rope(proj(1), c, a, bt)
    for h in range(N_HEADS):
        ak_ref[:, h * HEAD_W:(h + 1) * HEAD_W] = k[h].astype(BF16)
    v = proj(2).astype(BF16)
    ones = jnp.ones((v.shape[0], HEAD_W), BF16)
    for h in range(N_HEADS):
        av_ref[:, 2 * h * HEAD_W:(2 * h + 1) * HEAD_W] = v[:, h * HEAD_W:(h + 1) * HEAD_W]
        av_ref[:, (2 * h + 1) * HEAD_W:(2 * h + 2) * HEAD_W] = ones
    hq = proj(3)
    hq_ref[...] = hq * _sigmoid(hq)
    gf_ref[...] = _log_forget(proj(4), lbf_ref[...])
    gb_ref[...] = _log_forget(proj(5), lbb_ref[...])
    hi_ref[...] = proj(6).astype(BF16)
    gt = proj(7)
    gate_ref[...] = gt * _sigmoid(gt)


def _in_proj(x, w_in, rope_c, rope_a, rope_b, lb_f, lb_b, seq):
    t = x.shape[0]
    nseq = seq // ROW_TILE
    row = lambda i: (i, 0)
    const = lambda i: (0, 0)
    pos = lambda i: (i % nseq, 0)
    grp = pl.BlockSpec((ROW_TILE, 512), row)
    shp = lambda dt: jax.ShapeDtypeStruct((t, 512), dt)
    return pl.pallas_call(
        _in_proj_kernel,
        grid=(t // ROW_TILE,),
        in_specs=[
            pl.BlockSpec((ROW_TILE, D_MODEL), row),
            pl.BlockSpec((D_MODEL, IN_WIDTH), const, pipeline_mode=pl.Buffered(1)),
            pl.BlockSpec((ROW_TILE, HEAD_W), pos),
            pl.BlockSpec((ROW_TILE, HEAD_W), pos),
            pl.BlockSpec((ROW_TILE, HEAD_W), pos),
            pl.BlockSpec((1, HG_WIDTH), const),
            pl.BlockSpec((1, HG_WIDTH), const),
        ],
        out_specs=[grp, grp, pl.BlockSpec((ROW_TILE, 2 * ATTN_WIDTH), row), grp, grp, grp, grp, grp],
        out_shape=[shp(BF16), shp(BF16), jax.ShapeDtypeStruct((t, 2 * ATTN_WIDTH), BF16),
                   shp(F32), shp(F32), shp(F32), shp(BF16), shp(F32)],
        compiler_params=pltpu.CompilerParams(
            dimension_semantics=("arbitrary",), vmem_limit_bytes=VMEM_LIMIT),
        name="in_proj",
    )(x, w_in, rope_c, rope_a, rope_b, lb_f, lb_b)


def _attn_kernel(lam_ref, q_ref, k_ref, v_ref, g_ref, o_ref, sa_ref, sb_ref, m_ref, acc_ref, *, seq, out_scale):
    q = q_ref[...]
    lane = lax.broadcasted_iota(jnp.int32, q.shape, 1)
    zero = jnp.zeros_like(q)
    q1 = jnp.where(lane < ATTN_HEAD_DIM, q, zero)
    q2 = jnp.where(lane >= ATTN_HEAD_DIM, q, zero)
    nt = (((1,), (1,)), ((), ()))
    tq = q.shape[0]
    n_chunks = seq // ATTN_TK

    def chunk(j):
        return pl.ds(pl.multiple_of(j * ATTN_TK, ATTN_TK), ATTN_TK)

    def scores(j, s_ref):
        k = k_ref[chunk(j), :]
        s_ref[0] = lax.dot_general(q1, k, nt, preferred_element_type=F32)
        s_ref[1] = lax.dot_general(q2, k, nt, preferred_element_type=F32)

    def accumulate(j, s_ref):
        v = v_ref[chunk(j), :]
        for c in range(2):
            s = s_ref[c]
            m = m_ref[c]
            m_new = jnp.maximum(m, jnp.max(s, axis=1, keepdims=True))
            e = jnp.exp2(s - m_new).astype(BF16)
            m_ref[c] = m_new
            acc_ref[c] = jnp.exp2(m - m_new) * acc_ref[c] + jnp.dot(e, v, preferred_element_type=F32)

    m_ref[...] = jnp.full(m_ref.shape, -jnp.inf, F32)
    acc_ref[...] = jnp.zeros(acc_ref.shape, F32)
    scores(0, sa_ref)

    def body(i, carry):
        j = 2 * i
        scores(j + 1, sb_ref)
        accumulate(j, sa_ref)
        scores(j + 2, sa_ref)
        accumulate(j + 1, sb_ref)
        return carry

    lax.fori_loop(0, n_chunks // 2 - 1, body, 0)
    scores(n_chunks - 1, sb_ref)
    accumulate(n_chunks - 2, sa_ref)
    accumulate(n_chunks - 1, sb_ref)
    lam = lam_ref[0]
    a1, a2 = acc_ref[0], acc_ref[1]
    o = a1[:, :HEAD_W] / a1[:, HEAD_W:HEAD_W + 1] - lam * (a2[:, :HEAD_W] / a2[:, HEAD_W:HEAD_W + 1])
    ms = jnp.mean(o * o, axis=-1, keepdims=True)
    o_ref[...] = (o * lax.rsqrt(ms + EPS) * g_ref[...] * out_scale).astype(BF16)


def _attn(lam, aq, ak, av, g, batch, seq, out_scale):
    t = aq.shape[0]
    nq = seq // ATTN_TQ
    return pl.pallas_call(
        functools.partial(_attn_kernel, seq=seq, out_scale=out_scale),
        grid=(batch, N_HEADS, nq),
        in_specs=[
            pl.BlockSpec(memory_space=pltpu.SMEM),
            pl.BlockSpec((ATTN_TQ, HEAD_W), lambda b, h, i: (b * nq + i, h)),
            pl.BlockSpec((seq, HEAD_W), lambda b, h, i: (b, h)),
            pl.BlockSpec((seq, 2 * HEAD_W), lambda b, h, i: (b, h)),
            pl.BlockSpec((1, HEAD_W), lambda b, h, i: (0, 0)),
        ],
        out_specs=pl.BlockSpec((ATTN_TQ, HEAD_W), lambda b, h, i: (b * nq + i, h)),
        out_shape=jax.ShapeDtypeStruct((t, ATTN_WIDTH), BF16),
        scratch_shapes=[
            pltpu.VMEM((2, ATTN_TQ, ATTN_TK), F32),
            pltpu.VMEM((2, ATTN_TQ, ATTN_TK), F32),
            pltpu.VMEM((2, ATTN_TQ, 1), F32),
            pltpu.VMEM((2, ATTN_TQ, 2 * HEAD_W), F32),
        ],
        compiler_params=pltpu.CompilerParams(
            dimension_semantics=("arbitrary", "arbitrary", "arbitrary"), vmem_limit_bytes=VMEM_LIMIT),
        name="attn",
    )(lam, aq, ak, av, g)


def _cumsum16(g, reverse):
    row = lax.broadcasted_iota(jnp.int32, g.shape, 0)
    b = g
    for sh in (1, 2, 4, 8):
        if reverse:
            b = b + jnp.where(row < HG_STEP - sh, pltpu.roll(b, HG_STEP - sh, axis=0), 0.0)
        else:
            b = b + jnp.where(row >= sh, pltpu.roll(b, sh, axis=0), 0.0)
    return b


def _hgrn_step(q, g, v, st, reverse):
    kk = 1.0 - jnp.exp(g)
    b = _cumsum16(g, reverse)
    vf = v.astype(F32)
    row = lax.broadcasted_iota(jnp.int32, g.shape, 0)
    qd = (q * jnp.exp(b)).astype(BF16)
    o = lax.dot_general(qd, st.astype(BF16), (((1,), (1,)), ((), ())), preferred_element_type=F32)
    for s in range(HG_STEP):
        mask = (row <= s) if reverse else (row >= s)
        d = jnp.where(mask, b - b[s:s + 1, :], -jnp.inf)
        w = q * jnp.exp(d) * kk[s:s + 1, :]
        o = o + jnp.sum(w, axis=1, keepdims=True) * vf[s:s + 1, :]
    b_end = b[0:1, :] if reverse else b[HG_STEP - 1:HG_STEP, :]
    kd = (kk * jnp.exp(b_end - b)).astype(BF16)
    upd = lax.dot_general(v, kd, (((0,), (0,)), ((), ())), preferred_element_type=F32)
    return o, st * jnp.exp(b_end) + upd


def _hgrn_kernel(qf_ref, gf_ref, vf_ref, qb_ref, gb_ref, vb_ref, of_ref, ob_ref, sf_ref, sb_ref):
    @pl.when(pl.program_id(1) == 0)
    def _():
        sf_ref[...] = jnp.zeros_like(sf_ref)
        sb_ref[...] = jnp.zeros_like(sb_ref)

    nstep = HG_BLOCK // HG_STEP

    def body(i, carry):
        rf = pl.multiple_of(i * HG_STEP, HG_STEP)
        rb = pl.multiple_of((nstep - 1 - i) * HG_STEP, HG_STEP)
        for h in range(N_HEADS):
            cols = slice(h * HEAD_W, (h + 1) * HEAD_W)
            o, st = _hgrn_step(qf_ref[pl.ds(rf, HG_STEP), cols], gf_ref[pl.ds(rf, HG_STEP), cols],
                               vf_ref[pl.ds(rf, HG_STEP), cols], sf_ref[h], False)
            of_ref[pl.ds(rf, HG_STEP), cols] = o
            sf_ref[h] = st
            o, st = _hgrn_step(qb_ref[pl.ds(rb, HG_STEP), cols], gb_ref[pl.ds(rb, HG_STEP), cols],
                               vb_ref[pl.ds(rb, HG_STEP), cols], sb_ref[h], True)
            ob_ref[pl.ds(rb, HG_STEP), cols] = o
            sb_ref[h] = st
        return carry

    lax.fori_loop(0, nstep, body, 0)


def _hgrn(hq, gf, gb, hi, batch, seq):
    t = hq.shape[0]
    nb = seq // HG_BLOCK
    fwd = pl.BlockSpec((HG_BLOCK, HG_WIDTH), lambda b, j: (b * nb + j, 0))
    bwd = pl.BlockSpec((HG_BLOCK, HG_WIDTH), lambda b, j: (b * nb + nb - 1 - j, 0))
    return pl.pallas_call(
        _hgrn_kernel,
        grid=(batch, nb),
        in_specs=[fwd, fwd, fwd, bwd, bwd, bwd],
        out_specs=[fwd, bwd],
        out_shape=[jax.ShapeDtypeStruct((t, HG_WIDTH), F32)] * 2,
        scratch_shapes=[pltpu.VMEM((N_HEADS, HEAD_W, HEAD_W), F32)] * 2,
        compiler_params=pltpu.CompilerParams(
            dimension_semantics=("arbitrary", "arbitrary"), vmem_limit_bytes=VMEM_LIMIT),
        name="hgrn",
    )(hq, gf, hi, hq, gb, hi)


def _out_proj_kernel(x_ref, ao_ref, of_ref, ob_ref, gate_ref, hgg_ref, w_ref, g_ref, b_ref, o_ref):
    x = x_ref[...]
    ho = of_ref[...] + ob_ref[...]
    gate = gate_ref[...]
    hgg = hgg_ref[...]
    mix = jnp.dot(ao_ref[...], w_ref[0:ATTN_WIDTH, :], preferred_element_type=F32)
    for h in range(N_HEADS):
        cols = slice(h * HEAD_W, (h + 1) * HEAD_W)
        oh = ho[:, cols]
        ms = jnp.mean(oh * oh, axis=-1, keepdims=True)
        nh = (oh * lax.rsqrt(ms + EPS) * hgg * gate[:, cols]).astype(BF16)
        mix = mix + jnp.dot(nh, w_ref[ATTN_WIDTH + h * HEAD_W:ATTN_WIDTH + (h + 1) * HEAD_W, :],
                            preferred_element_type=F32)
    y = ALPHA * x + mix
    o_ref[...] = _layer_norm(y, g_ref[...], b_ref[...])


def _out_proj(x, ao, of, ob, gate, hgg, w_out, g, b):
    t = x.shape[0]
    row = lambda i: (i, 0)
    const = lambda i: (0, 0)
    grp = pl.BlockSpec((ROW_TILE, 512), row)
    return pl.pallas_call(
        _out_proj_kernel,
        grid=(t // ROW_TILE,),
        in_specs=[
            pl.BlockSpec((ROW_TILE, D_MODEL), row),
            grp, grp, grp, grp,
            pl.BlockSpec((1, HEAD_W), const),
            pl.BlockSpec((D_MODEL, D_MODEL), const, pipeline_mode=pl.Buffered(1)),
            pl.BlockSpec((1, D_MODEL), const),
            pl.BlockSpec((1, D_MODEL), const),
        ],
        out_specs=pl.BlockSpec((ROW_TILE, D_MODEL), row),
        out_shape=jax.ShapeDtypeStruct((t, D_MODEL), F32),
        compiler_params=pltpu.CompilerParams(
            dimension_semantics=("arbitrary",), vmem_limit_bytes=VMEM_LIMIT),
        name="out_proj",
    )(x, ao, of, ob, gate, hgg, w_out, g, b)


def _rope_tables(seq):
    half = ROT_DIM // 2
    inv_freq = jnp.float32(ROPE_THETA) ** (-jnp.arange(half, dtype=F32) * 2.0 / ROT_DIM)
    ang = jnp.arange(seq, dtype=F32)[:, None] * inv_freq[None, :]
    cos, sin = jnp.cos(ang), jnp.sin(ang)
    ones = jnp.ones((seq, ATTN_HEAD_DIM - ROT_DIM), F32)
    zeros = jnp.zeros((seq, ATTN_HEAD_DIM - ROT_DIM), F32)
    z8 = jnp.zeros((seq, half), F32)
    comp_c = jnp.concatenate([cos, cos, ones], axis=1)
    comp_a = jnp.concatenate([-sin, z8, zeros], axis=1)
    comp_b = jnp.concatenate([z8, sin, zeros], axis=1)
    two = lambda m: jnp.concatenate([m, m], axis=1)
    return two(comp_c), two(comp_a), two(comp_b)


def _trunk(x3, p):
    batch, seq, _ = x3.shape
    x = x3.reshape(batch * seq, D_MODEL)
    rope_c, rope_a, rope_b = _rope_tables(seq)
    for l in range(DEPTH):
        lam_init = 0.8 - 0.6 * math.exp(-0.3 * l)
        x = _ffn(x, p["wg"][l][0], p["wu"][l][0], p["wd"][l][0], p["ln_g"][l][0], p["ln_b"][l][0])
        aq, ak, av, hq, gf, gb, hi, gate = _in_proj(
            x, p["w_in"][l], rope_c, rope_a, rope_b, p["lbs"][0][l], p["lbs"][1][l], seq)
        ao = _attn(p["lam"][l], aq, ak, av, p["attn_g"][l], batch, seq, 1.0 - lam_init)
        of, ob = _hgrn(hq, gf, gb, hi, batch, seq)
        x = _out_proj(x, ao, of, ob, gate, p["hg_g"][l], p["w_out"][l], p["ln_g"][l][1], p["ln_b"][l][1])
        x = _ffn(x, p["wg"][l][1], p["wu"][l][1], p["wd"][l][1], p["ln_g"][l][2], p["ln_b"][l][2])
    return x.reshape(batch, seq, D_MODEL)


def kernel(x_prompt, x_sample, w_in, w_out, attn_lambda, attn_norm_g, hg_norm_g, hg_lower_bound,
           ffn_w_gate, ffn_w_up, ffn_w_down, ln_g, ln_b):
    sm = jax.nn.softmax(hg_lower_bound.astype(F32), axis=1)
    lbs = jnp.maximum(jnp.cumsum(sm, axis=1) - sm[:, :1], 0.0)
    lp = attn_lambda.astype(F32)
    lam_init = jnp.asarray([0.8 - 0.6 * math.exp(-0.3 * l) for l in range(DEPTH)], F32)
    lam = (jnp.exp(jnp.sum(lp[:, 0] * lp[:, 1], axis=-1))
           - jnp.exp(jnp.sum(lp[:, 2] * lp[:, 3], axis=-1)) + lam_init)
    p = {
        "w_in": w_in.astype(BF16), "w_out": w_out.astype(BF16),
        "wg": ffn_w_gate.astype(BF16), "wu": ffn_w_up.astype(BF16), "wd": ffn_w_down.astype(BF16),
        "ln_g": ln_g.reshape(DEPTH, 3, 1, D_MODEL), "ln_b": ln_b.reshape(DEPTH, 3, 1, D_MODEL),
        "attn_g": attn_norm_g.reshape(DEPTH, 1, HEAD_W), "hg_g": hg_norm_g.reshape(DEPTH, 1, HEAD_W),
        "lbs": lbs.reshape(2, DEPTH, 1, HG_WIDTH), "lam": lam.reshape(DEPTH, 1),
    }
    return (_trunk(x_prompt, p), _trunk(x_sample, p))
```

```python
import functools
import math

import jax
import jax.numpy as jnp
from jax import lax
from jax.experimental import pallas as pl
from jax.experimental.pallas import tpu as pltpu

D_MODEL = 1024
DEPTH = 2
ATTN_WIDTH = 512
HG_WIDTH = 512
HEAD_W = 128
N_HEADS = 4
ATTN_HEAD_DIM = 64
ROT_DIM = 16
ROPE_THETA = 500000.0
D_FF = 2816
ALPHA = (2 * DEPTH) ** 0.25
EPS = 1e-5
IN_WIDTH = 3 * ATTN_WIDTH + 5 * HG_WIDTH
LOG2E = math.log2(math.e)
ATTN_Q_SCALE = ATTN_HEAD_DIM ** -0.5 * LOG2E

ROW_TILE = 512
FF_CHUNK = 512
ATTN_TQ = 512
ATTN_TK = 512
HG_BLOCK = 256
HG_STEP = 16
VMEM_LIMIT = 56 * 1024 * 1024

BF16 = jnp.bfloat16
F32 = jnp.float32


def _layer_norm(y, g, b):
    mu = jnp.mean(y, axis=-1, keepdims=True)
    yc = y - mu
    var = jnp.mean(yc * yc, axis=-1, keepdims=True)
    return yc * lax.rsqrt(var + EPS) * g + b


def _sigmoid(z):
    return 1.0 / (1.0 + jnp.exp(-z))


def _ffn_kernel(x_ref, wg_ref, wu_ref, wd_ref, g_ref, b_ref, o_ref):
    x = x_ref[...]
    xb = x.astype(BF16)
    acc = jnp.zeros((x.shape[0], D_MODEL), F32)
    for f0 in range(0, D_FF, FF_CHUNK):
        f1 = min(f0 + FF_CHUNK, D_FF)
        hg = jnp.dot(xb, wg_ref[:, f0:f1], preferred_element_type=F32)
        hu = jnp.dot(xb, wu_ref[:, f0:f1], preferred_element_type=F32)
        h = (hg * _sigmoid(hg) * hu).astype(BF16)
        acc = acc + jnp.dot(h, wd_ref[f0:f1, :], preferred_element_type=F32)
    y = ALPHA * x + 0.5 * acc
    o_ref[...] = _layer_norm(y, g_ref[...], b_ref[...])


def _ffn(x, wg, wu, wd, g, b):
    t = x.shape[0]
    row = lambda i: (i, 0)
    const = lambda i: (0, 0)
    return pl.pallas_call(
        _ffn_kernel,
        grid=(t // ROW_TILE,),
        in_specs=[
            pl.BlockSpec((ROW_TILE, D_MODEL), row),
            pl.BlockSpec((D_MODEL, D_FF), const, pipeline_mode=pl.Buffered(1)),
            pl.BlockSpec((D_MODEL, D_FF), const, pipeline_mode=pl.Buffered(1)),
            pl.BlockSpec((D_FF, D_MODEL), const, pipeline_mode=pl.Buffered(1)),
            pl.BlockSpec((1, D_MODEL), const),
            pl.BlockSpec((1, D_MODEL), const),
        ],
        out_specs=pl.BlockSpec((ROW_TILE, D_MODEL), row),
        out_shape=jax.ShapeDtypeStruct((t, D_MODEL), F32),
        compiler_params=pltpu.CompilerParams(
            dimension_semantics=("arbitrary",), vmem_limit_bytes=VMEM_LIMIT),
        name="ffn",
    )(x, wg, wu, wd, g, b)


def _rope(x, c, a, bt):
    outs = []
    for h in range(N_HEADS):
        xh = x[:, h * HEAD_W:(h + 1) * HEAD_W]
        up = pltpu.roll(xh, HEAD_W - ROT_DIM // 2, axis=1)
        dn = pltpu.roll(xh, ROT_DIM // 2, axis=1)
        outs.append(xh * c + up * a + dn * bt)
    return outs


def _log_forget(z, lb):
    log_f = jnp.logaddexp(jnp.log(lb), jnp.log1p(-lb) + jax.nn.log_sigmoid(z))
    return jnp.minimum(log_f, 0.0) * LOG2E


def _in_proj_kernel(x_ref, w_ref, c_ref, a_ref, bt_ref, lbf_ref, lbb_ref,
                    aq_ref, ak_ref, av_ref, hq_ref, gf_ref, gb_ref, hi_ref, gate_ref):
    xb = x_ref[...].astype(BF16)
    c, a, bt = c_ref[...], a_ref[...], bt_ref[...]

    def proj(j):
        return jnp.dot(xb, w_ref[:, j * 512:(j + 1) * 512], preferred_element_type=F32)

    q = _rope(proj(0), c, a, bt)
    for h in range(N_HEADS):
        aq_ref[:, h * HEAD_W:(h + 1) * HEAD_W] = (q[h] * ATTN_Q_SCALE).astype(BF16)
    k = _rope(proj(1), c, a, bt)
    for h in range(N_HEADS):
        ak_ref[:, h * HEAD_W:(h + 1) * HEAD_W] = k[h].astype(BF16)
    v = proj(2).astype(BF16)
    ones = jnp.ones((v.shape[0], HEAD_W), BF16)
    for h in range(N_HEADS):
        av_ref[:, 2 * h * HEAD_W:(2 * h + 1) * HEAD_W] = v[:, h * HEAD_W:(h + 1) * HEAD_W]
        av_ref[:, (2 * h + 1) * HEAD_W:(2 * h + 2) * HEAD_W] = ones
    hq = proj(3)
    hq_ref[...] = hq * _sigmoid(hq)
    gf_ref[...] = _log_forget(proj(4), lbf_ref[...])
    gb_ref[...] = _log_forget(proj(5), lbb_ref[...])
    hi_ref[...] = proj(6).astype(BF16)
    gt = proj(7)
    gate_ref[...] = gt * _sigmoid(gt)


def _in_proj(x, w_in, rope_c, rope_a, rope_b, lb_f, lb_b, seq):
    t = x.shape[0]
    nseq = seq // ROW_TILE
    row = lambda i: (i, 0)
    const = lambda i: (0, 0)
    pos = lambda i: (i % nseq, 0)
    grp = pl.BlockSpec((ROW_TILE, 512), row)
    shp = lambda dt: jax.ShapeDtypeStruct((t, 512), dt)
    return pl.pallas_call(
        _in_proj_kernel,
        grid=(t // ROW_TILE,),
        in_specs=[
            pl.BlockSpec((ROW_TILE, D_MODEL), row),
            pl.BlockSpec((D_MODEL, IN_WIDTH), const, pipeline_mode=pl.Buffered(1)),
            pl.BlockSpec((ROW_TILE, HEAD_W), pos),
            pl.BlockSpec((ROW_TILE, HEAD_W), pos),
            pl.BlockSpec((ROW_TILE, HEAD_W), pos),
            pl.BlockSpec((1, HG_WIDTH), const),
            pl.BlockSpec((1, HG_WIDTH), const),
        ],
        out_specs=[grp, grp, pl.BlockSpec((ROW_TILE, 2 * ATTN_WIDTH), row), grp, grp, grp, grp, grp],
        out_shape=[shp(BF16), shp(BF16), jax.ShapeDtypeStruct((t, 2 * ATTN_WIDTH), BF16),
                   shp(F32), shp(F32), shp(F32), shp(BF16), shp(F32)],
        compiler_params=pltpu.CompilerParams(
            dimension_semantics=("arbitrary",), vmem_limit_bytes=VMEM_LIMIT),
        name="in_proj",
    )(x, w_in, rope_c, rope_a, rope_b, lb_f, lb_b)


def _attn_kernel(lam_ref, q_ref, k_ref, v_ref, g_ref, o_ref, sa_ref, sb_ref, m_ref, acc_ref, *, seq, out_scale):
    q = q_ref[...]
    lane = lax.broadcasted_iota(jnp.int32, q.shape, 1)
    zero = jnp.zeros_like(q)
    q1 = jnp.where(lane < ATTN_HEAD_DIM, q, zero)
    q2 = jnp.where(lane >= ATTN_HEAD_DIM, q, zero)
    nt = (((1,), (1,)), ((), ()))
    tq = q.shape[0]
    n_chunks = seq // ATTN_TK

    def chunk(j):
        return pl.ds(pl.multiple_of(j * ATTN_TK, ATTN_TK), ATTN_TK)

    def scores(j, s_ref):
        k = k_ref[chunk(j), :]
        s_ref[0] = lax.dot_general(q1, k, nt, preferred_element_type=F32)
        s_ref[1] = lax.dot_general(q2, k, nt, preferred_element_type=F32)

    def accumulate(j, s_ref):
        v = v_ref[chunk(j), :]
        for c in range(2):
            s = s_ref[c]
            m = m_ref[c]
            m_new = jnp.maximum(m, jnp.max(s, axis=1, keepdims=True))
            e = jnp.exp2(s - m_new).astype(BF16)
            m_ref[c] = m_new
            acc_ref[c] = jnp.exp2(m - m_new) * acc_ref[c] + jnp.dot(e, v, preferred_element_type=F32)

    m_ref[...] = jnp.full(m_ref.shape, -jnp.inf, F32)
    acc_ref[...] = jnp.zeros(acc_ref.shape, F32)
    scores(0, sa_ref)

    def body(i, carry):
        j = 2 * i
        scores(j + 1, sb_ref)
        accumulate(j, sa_ref)
        scores(j + 2, sa_ref)
        accumulate(j + 1, sb_ref)
        return carry

    lax.fori_loop(0, n_chunks // 2 - 1, body, 0)
    scores(n_chunks - 1, sb_ref)
    accumulate(n_chunks - 2, sa_ref)
    accumulate(n_chunks - 1, sb_ref)
    lam = lam_ref[0]
    a1, a2 = acc_ref[0], acc_ref[1]
    o = a1[:, :HEAD_W] / a1[:, HEAD_W:HEAD_W + 1] - lam * (a2[:, :HEAD_W] / a2[:, HEAD_W:HEAD_W + 1])
    ms = jnp.mean(o * o, axis=-1, keepdims=True)
    o_ref[...] = (o * lax.rsqrt(ms + EPS) * g_ref[...] * out_scale).astype(BF16)


def _attn(lam, aq, ak, av, g, batch, seq, out_scale):
    t = aq.shape[0]
    nq = seq // ATTN_TQ
    return pl.pallas_call(
        functools.partial(_attn_kernel, seq=seq, out_scale=out_scale),
        grid=(batch, N_HEADS, nq),
        in_specs=[
            pl.BlockSpec(memory_space=pltpu.SMEM),
            pl.BlockSpec((ATTN_TQ, HEAD_W), lambda b, h, i: (b * nq + i, h)),
            pl.BlockSpec((seq, HEAD_W), lambda b, h, i: (b, h)),
            pl.BlockSpec((seq, 2 * HEAD_W), lambda b, h, i: (b, h)),
            pl.BlockSpec((1, HEAD_W), lambda b, h, i: (0, 0)),
        ],
        out_specs=pl.BlockSpec((ATTN_TQ, HEAD_W), lambda b, h, i: (b * nq + i, h)),
        out_shape=jax.ShapeDtypeStruct((t, ATTN_WIDTH), BF16),
        scratch_shapes=[
            pltpu.VMEM((2, ATTN_TQ, ATTN_TK), F32),
            pltpu.VMEM((2, ATTN_TQ, ATTN_TK), F32),
            pltpu.VMEM((2, ATTN_TQ, 1), F32),
            pltpu.VMEM((2, ATTN_TQ, 2 * HEAD_W), F32),
        ],
        compiler_params=pltpu.CompilerParams(
            dimension_semantics=("arbitrary", "arbitrary", "arbitrary"), vmem_limit_bytes=VMEM_LIMIT),
        name="attn",
    )(lam, aq, ak, av, g)


def _cumsum16(g, reverse):
    row = lax.broadcasted_iota(jnp.int32, g.shape, 0)
    b = g
    for sh in (1, 2, 4, 8):
        if reverse:
            b = b + jnp.where(row < HG_STEP - sh, pltpu.roll(b, HG_STEP - sh, axis=0), 0.0)
        else:
            b = b + jnp.where(row >= sh, pltpu.roll(b, sh, axis=0), 0.0)
    return b


def _hgrn_state(q, kk, b, v, st, reverse):
    qd = (q * jnp.exp2(b)).astype(BF16)
    o = lax.dot_general(qd, st.astype(BF16), (((1,), (1,)), ((), ())), preferred_element_type=F32)
    b_end = b[0:1, :] if reverse else b[HG_STEP - 1:HG_STEP, :]
    kd = (kk * jnp.exp2(b_end - b)).astype(BF16)
    upd = lax.dot_general(v, kd, (((0,), (0,)), ((), ())), preferred_element_type=F32)
    return o, st * jnp.exp2(b_end) + upd


def _hgrn_pairwise(q, kk, b, v, reverse):
    c = b - jnp.log2(kk)
    half = HG_STEP // 2
    row = lax.broadcasted_iota(jnp.int32, (half, HEAD_W), 0)
    vf = v.astype(F32)
    o = [jnp.zeros((half, HEAD_W), F32), jnp.zeros((half, HEAD_W), F32)]
    for s in range(HG_STEP):
        cs = c[s:s + 1, :]
        vs = vf[s:s + 1, :]
        for hi in range(2):
            r0 = hi * half
            if (r0 > s) if reverse else (r0 + half - 1 < s):
                continue
            d = b[r0:r0 + half, :] - cs
            if (r0 + half - 1 > s) if reverse else (r0 < s):
                seen = (row + r0 <= s) if reverse else (row + r0 >= s)
                d = jnp.where(seen, d, -jnp.inf)
            col = jnp.sum(q[r0:r0 + half, :] * jnp.exp2(d), axis=1, keepdims=True)
            o[hi] = o[hi] + col * vs
    return jnp.concatenate(o, axis=0)


def _hgrn_kernel(qf_ref, gf_ref, vf_ref, qb_ref, gb_ref, vb_ref, of_ref, ob_ref, sf_ref, sb_ref):
    @pl.when(pl.program_id(1) == 0)
    def _():
        sf_ref[...] = jnp.zeros_like(sf_ref)
        sb_ref[...] = jnp.zeros_like(sb_ref)

    nstep = HG_BLOCK // HG_STEP

    def body(i, carry):
        rf = pl.multiple_of(i * HG_STEP, HG_STEP)
        rb = pl.multiple_of((nstep - 1 - i) * HG_STEP, HG_STEP)
        chains = []
        for h in range(N_HEADS):
            cols = slice(h * HEAD_W, (h + 1) * HEAD_W)
            chains.append((qf_ref, gf_ref, vf_ref, of_ref, sf_ref, rf, h, cols, False))
            chains.append((qb_ref, gb_ref, vb_ref, ob_ref, sb_ref, rb, h, cols, True))
        live = []
        for q_ref, g_ref, v_ref, o_ref, s_ref, r, h, cols, reverse in chains:
            rows = pl.ds(r, HG_STEP)
            q, g, v = q_ref[rows, cols], g_ref[rows, cols], v_ref[rows, cols]
            kk = 1.0 - jnp.exp2(g)
            b = _cumsum16(g, reverse)
            o, st = _hgrn_state(q, kk, b, v, s_ref[h], reverse)
            o_ref[rows, cols] = o
            s_ref[h] = st
            live.append((q, kk, b, v))
        for (q_ref, g_ref, v_ref, o_ref, s_ref, r, h, cols, reverse), (q, kk, b, v) in zip(chains, live):
            rows = pl.ds(r, HG_STEP)
            o_ref[rows, cols] += _hgrn_pairwise(q, kk, b, v, reverse)
        return carry

    lax.fori_loop(0, nstep, body, 0)


def _hgrn(hq, gf, gb, hi, batch, seq):
    t = hq.shape[0]
    nb = seq // HG_BLOCK
    fwd = pl.BlockSpec((HG_BLOCK, HG_WIDTH), lambda b, j: (b * nb + j, 0))
    bwd = pl.BlockSpec((HG_BLOCK, HG_WIDTH), lambda b, j: (b * nb + nb - 1 - j, 0))
    return pl.pallas_call(
        _hgrn_kernel,
        grid=(batch, nb),
        in_specs=[fwd, fwd, fwd, bwd, bwd, bwd],
        out_specs=[fwd, bwd],
        out_shape=[jax.ShapeDtypeStruct((t, HG_WIDTH), F32)] * 2,
        scratch_shapes=[pltpu.VMEM((N_HEADS, HEAD_W, HEAD_W), F32)] * 2,
        compiler_params=pltpu.CompilerParams(
            dimension_semantics=("arbitrary", "arbitrary"), vmem_limit_bytes=VMEM_LIMIT),
        name="hgrn",
    )(hq, gf, hi, hq, gb, hi)


def _out_proj_kernel(x_ref, ao_ref, of_ref, ob_ref, gate_ref, hgg_ref, w_ref, g_ref, b_ref, o_ref):
    x = x_ref[...]
    ho = of_ref[...] + ob_ref[...]
    gate = gate_ref[...]
    hgg = hgg_ref[...]
    mix = jnp.dot(ao_ref[...], w_ref[0:ATTN_WIDTH, :], preferred_element_type=F32)
    for h in range(N_HEADS):
        cols = slice(h * HEAD_W, (h + 1) * HEAD_W)
        oh = ho[:, cols]
        ms = jnp.mean(oh * oh, axis=-1, keepdims=True)
        nh = (oh * lax.rsqrt(ms + EPS) * hgg * gate[:, cols]).astype(BF16)
        mix = mix + jnp.dot(nh, w_ref[ATTN_WIDTH + h * HEAD_W:ATTN_WIDTH + (h + 1) * HEAD_W, :],
                            preferred_element_type=F32)
    y = ALPHA * x + mix
    o_ref[...] = _layer_norm(y, g_ref[...], b_ref[...])


def _out_proj(x, ao, of, ob, gate, hgg, w_out, g, b):
    t = x.shape[0]
    row = lambda i: (i, 0)
    const = lambda i: (0, 0)
    grp = pl.BlockSpec((ROW_TILE, 512), row)
    return pl.pallas_call(
        _out_proj_kernel,
        grid=(t // ROW_TILE,),
        in_specs=[
            pl.BlockSpec((ROW_TILE, D_MODEL), row),
            grp, grp, grp, grp,
            pl.BlockSpec((1, HEAD_W), const),
            pl.BlockSpec((D_MODEL, D_MODEL), const, pipeline_mode=pl.Buffered(1)),
            pl.BlockSpec((1, D_MODEL), const),
            pl.BlockSpec((1, D_MODEL), const),
        ],
        out_specs=pl.BlockSpec((ROW_TILE, D_MODEL), row),
        out_shape=jax.ShapeDtypeStruct((t, D_MODEL), F32),
        compiler_params=pltpu.CompilerParams(
            dimension_semantics=("arbitrary",), vmem_limit_bytes=VMEM_LIMIT),
        name="out_proj",
    )(x, ao, of, ob, gate, hgg, w_out, g, b)


def _rope_tables(seq):
    half = ROT_DIM // 2
    inv_freq = jnp.float32(ROPE_THETA) ** (-jnp.arange(half, dtype=F32) * 2.0 / ROT_DIM)
    ang = jnp.arange(seq, dtype=F32)[:, None] * inv_freq[None, :]
    cos, sin = jnp.cos(ang), jnp.sin(ang)
    ones = jnp.ones((seq, ATTN_HEAD_DIM - ROT_DIM), F32)
    zeros = jnp.zeros((seq, ATTN_HEAD_DIM - ROT_DIM), F32)
    z8 = jnp.zeros((seq, half), F32)
    comp_c = jnp.concatenate([cos, cos, ones], axis=1)
    comp_a = jnp.concatenate([-sin, z8, zeros], axis=1)
    comp_b = jnp.concatenate([z8, sin, zeros], axis=1)
    two = lambda m: jnp.concatenate([m, m], axis=1)
    return two(comp_c), two(comp_a), two(comp_b)


def _trunk(x3, p):
    batch, seq, _ = x3.shape
    x = x3.reshape(batch * seq, D_MODEL)
    rope_c, rope_a, rope_b = _rope_tables(seq)
    for l in range(DEPTH):
        lam_init = 0.8 - 0.6 * math.exp(-0.3 * l)
        x = _ffn(x, p["wg"][l][0], p["wu"][l][0], p["wd"][l][0], p["ln_g"][l][0], p["ln_b"][l][0])
        aq, ak, av, hq, gf, gb, hi, gate = _in_proj(
            x, p["w_in"][l], rope_c, rope_a, rope_b, p["lbs"][0][l], p["lbs"][1][l], seq)
        ao = _attn(p["lam"][l], aq, ak, av, p["attn_g"][l], batch, seq, 1.0 - lam_init)
        of, ob = _hgrn(hq, gf, gb, hi, batch, seq)
        x = _out_proj(x, ao, of, ob, gate, p["hg_g"][l], p["w_out"][l], p["ln_g"][l][1], p["ln_b"][l][1])
        x = _ffn(x, p["wg"][l][1], p["wu"][l][1], p["wd"][l][1], p["ln_g"][l][2], p["ln_b"][l][2])
    return x.reshape(batch, seq, D_MODEL)


def kernel(x_prompt, x_sample, w_in, w_out, attn_lambda, attn_norm_g, hg_norm_g, hg_lower_bound,
           ffn_w_gate, ffn_w_up, ffn_w_down, ln_g, ln_b):
    sm = jax.nn.softmax(hg_lower_bound.astype(F32), axis=1)
    lbs = jnp.maximum(jnp.cumsum(sm, axis=1) - sm[:, :1], 0.0)
    lp = attn_lambda.astype(F32)
    lam_init = jnp.asarray([0.8 - 0.6 * math.exp(-0.3 * l) for l in range(DEPTH)], F32)
    lam = (jnp.exp(jnp.sum(lp[:, 0] * lp[:, 1], axis=-1))
           - jnp.exp(jnp.sum(lp[:, 2] * lp[:, 3], axis=-1)) + lam_init)
    p = {
        "w_in": w_in.astype(BF16), "w_out": w_out.astype(BF16),
        "wg": ffn_w_gate.astype(BF16), "wu": ffn_w_up.astype(BF16), "wd": ffn_w_down.astype(BF16),
        "ln_g": ln_g.reshape(DEPTH, 3, 1, D_MODEL), "ln_b": ln_b.reshape(DEPTH, 3, 1, D_MODEL),
        "attn_g": attn_norm_g.reshape(DEPTH, 1, HEAD_W), "hg_g": hg_norm_g.reshape(DEPTH, 1, HEAD_W),
        "lbs": lbs.reshape(2, DEPTH, 1, HG_WIDTH), "lam": lam.reshape(DEPTH, 1),
    }
    return (_trunk(x_prompt, p), _trunk(x_sample, p))
```

```python
import functools
import math

import jax
import jax.numpy as jnp
from jax import lax
from jax.experimental import pallas as pl
from jax.experimental.pallas import tpu as pltpu

D_MODEL = 1024
DEPTH = 2
ATTN_WIDTH = 512
HG_WIDTH = 512
HEAD_W = 128
N_HEADS = 4
ATTN_HEAD_DIM = 64
ROT_DIM = 16
ROPE_THETA = 500000.0
D_FF = 2816
ALPHA = (2 * DEPTH) ** 0.25
EPS = 1e-5
IN_WIDTH = 3 * ATTN_WIDTH + 5 * HG_WIDTH
LOG2E = math.log2(math.e)
ATTN_Q_SCALE = ATTN_HEAD_DIM ** -0.5 * LOG2E

ROW_TILE = 512
FF_CHUNK = 512
ATTN_TQ = 512
ATTN_TK = 512
VT_ROWS = HEAD_W + 16
HG_BLOCK = 256
HG_STEP = 16
VMEM_LIMIT = 56 * 1024 * 1024

BF16 = jnp.bfloat16
F32 = jnp.float32


def _layer_norm(y, g, b):
    mu = jnp.mean(y, axis=-1, keepdims=True)
    yc = y - mu
    var = jnp.mean(yc * yc, axis=-1, keepdims=True)
    return yc * lax.rsqrt(var + EPS) * g + b


def _sigmoid(z):
    return 1.0 / (1.0 + jnp.exp(-z))


def _ffn_kernel(x_ref, wg_ref, wu_ref, wd_ref, g_ref, b_ref, o_ref):
    x = x_ref[...]
    xb = x.astype(BF16)
    acc = jnp.zeros((x.shape[0], D_MODEL), F32)
    for f0 in range(0, D_FF, FF_CHUNK):
        f1 = min(f0 + FF_CHUNK, D_FF)
        hg = jnp.dot(xb, wg_ref[:, f0:f1], preferred_element_type=F32)
        hu = jnp.dot(xb, wu_ref[:, f0:f1], preferred_element_type=F32)
        h = (hg * _sigmoid(hg) * hu).astype(BF16)
        acc = acc + jnp.dot(h, wd_ref[f0:f1, :], preferred_element_type=F32)
    y = ALPHA * x + 0.5 * acc
    o_ref[...] = _layer_norm(y, g_ref[...], b_ref[...])


def _ffn(x, wg, wu, wd, g, b):
    t = x.shape[0]
    row = lambda i: (i, 0)
    const = lambda i: (0, 0)
    return pl.pallas_call(
        _ffn_kernel,
        grid=(t // ROW_TILE,),
        in_specs=[
            pl.BlockSpec((ROW_TILE, D_MODEL), row),
            pl.BlockSpec((D_MODEL, D_FF), const, pipeline_mode=pl.Buffered(1)),
            pl.BlockSpec((D_MODEL, D_FF), const, pipeline_mode=pl.Buffered(1)),
            pl.BlockSpec((D_FF, D_MODEL), const, pipeline_mode=pl.Buffered(1)),
            pl.BlockSpec((1, D_MODEL), const),
            pl.BlockSpec((1, D_MODEL), const),
        ],
        out_specs=pl.BlockSpec((ROW_TILE, D_MODEL), row),
        out_shape=jax.ShapeDtypeStruct((t, D_MODEL), F32),
        compiler_params=pltpu.CompilerParams(
            dimension_semantics=("arbitrary",), vmem_limit_bytes=VMEM_LIMIT),
        name="ffn",
    )(x, wg, wu, wd, g, b)


def _rope(x, c, a, bt):
    outs = []
    for h in range(N_HEADS):
        xh = x[:, h * HEAD_W:(h + 1) * HEAD_W]
        up = pltpu.roll(xh, HEAD_W - ROT_DIM // 2, axis=1)
        dn = pltpu.roll(xh, ROT_DIM // 2, axis=1)
        outs.append(xh * c + up * a + dn * bt)
    return outs


def _log_forget(z, lb):
    z2 = z * LOG2E
    log_sig = jnp.minimum(z2, 0.0) - jnp.log2(1.0 + jnp.exp2(-jnp.abs(z2)))
    a = jnp.log2(lb)
    c = jnp.log2(1.0 - lb) + log_sig
    log_f = jnp.maximum(a, c) + jnp.log2(1.0 + jnp.exp2(-jnp.abs(a - c)))
    return jnp.minimum(log_f, 0.0)


def _in_proj_kernel(x_ref, w_ref, wvt_ref, c_ref, a_ref, bt_ref, lbf_ref, lbb_ref,
                    aq_ref, ak_ref, avt_ref, hq_ref, gf_ref, gb_ref, hi_ref, gate_ref):
    xb = x_ref[...].astype(BF16)
    c, a, bt = c_ref[...], a_ref[...], bt_ref[...]

    def proj(j):
        return jnp.dot(xb, w_ref[:, j * 512:(j + 1) * 512], preferred_element_type=F32)

    q = _rope(proj(0), c, a, bt)
    for h in range(N_HEADS):
        aq_ref[:, h * HEAD_W:(h + 1) * HEAD_W] = (q[h] * ATTN_Q_SCALE).astype(BF16)
    k = _rope(proj(1), c, a, bt)
    for h in range(N_HEADS):
        ak_ref[:, h * HEAD_W:(h + 1) * HEAD_W] = k[h].astype(BF16)
    vt = lax.dot_general(wvt_ref[...], xb, (((1,), (1,)), ((), ())), preferred_element_type=F32).astype(BF16)
    ones = jnp.ones((VT_ROWS - HEAD_W, vt.shape[1]), BF16)
    for h in range(N_HEADS):
        avt_ref[h * VT_ROWS:h * VT_ROWS + HEAD_W, :] = vt[h * HEAD_W:(h + 1) * HEAD_W, :]
        avt_ref[h * VT_ROWS + HEAD_W:(h + 1) * VT_ROWS, :] = ones
    hq = proj(3)
    hq_ref[...] = hq * _sigmoid(hq)
    gf_ref[...] = _log_forget(proj(4), lbf_ref[...])
    gb_ref[...] = _log_forget(proj(5), lbb_ref[...])
    hi_ref[...] = proj(6).astype(BF16)
    gt = proj(7)
    gate_ref[...] = gt * _sigmoid(gt)


def _in_proj(x, w_in, wvt, rope_c, rope_a, rope_b, lb_f, lb_b, seq):
    t = x.shape[0]
    nseq = seq // ROW_TILE
    row = lambda i: (i, 0)
    const = lambda i: (0, 0)
    pos = lambda i: (i % nseq, 0)
    grp = pl.BlockSpec((ROW_TILE, 512), row)
    shp = lambda dt: jax.ShapeDtypeStruct((t, 512), dt)
    return pl.pallas_call(
        _in_proj_kernel,
        grid=(t // ROW_TILE,),
        in_specs=[
            pl.BlockSpec((ROW_TILE, D_MODEL), row),
            pl.BlockSpec((D_MODEL, IN_WIDTH), const, pipeline_mode=pl.Buffered(1)),
            pl.BlockSpec((ATTN_WIDTH, D_MODEL), const, pipeline_mode=pl.Buffered(1)),
            pl.BlockSpec((ROW_TILE, HEAD_W), pos),
            pl.BlockSpec((ROW_TILE, HEAD_W), pos),
            pl.BlockSpec((ROW_TILE, HEAD_W), pos),
            pl.BlockSpec((1, HG_WIDTH), const),
            pl.BlockSpec((1, HG_WIDTH), const),
        ],
        out_specs=[grp, grp, pl.BlockSpec((N_HEADS * VT_ROWS, ROW_TILE), lambda i: (0, i)),
                   grp, grp, grp, grp, grp],
        out_shape=[shp(BF16), shp(BF16), jax.ShapeDtypeStruct((N_HEADS * VT_ROWS, t), BF16),
                   shp(F32), shp(F32), shp(F32), shp(BF16), shp(F32)],
        compiler_params=pltpu.CompilerParams(
            dimension_semantics=("arbitrary",), vmem_limit_bytes=VMEM_LIMIT),
        name="in_proj",
    )(x, w_in, wvt, rope_c, rope_a, rope_b, lb_f, lb_b)


def _attn_kernel(lam_ref, q_ref, k_ref, vt_ref, g_ref, o_ref, sa_ref, sb_ref, m_ref, acc_ref, *, seq, out_scale):
    q = q_ref[...]
    lane = lax.broadcasted_iota(jnp.int32, q.shape, 1)
    zero = jnp.zeros_like(q)
    qs = (jnp.where(lane < ATTN_HEAD_DIM, q, zero), jnp.where(lane >= ATTN_HEAD_DIM, q, zero))
    nt = (((1,), (1,)), ((), ()))
    n_chunks = seq // ATTN_TK

    def chunk(j):
        return pl.ds(pl.multiple_of(j * ATTN_TK, ATTN_TK), ATTN_TK)

    def scores(j, s_ref):
        k = k_ref[chunk(j), :]
        for c in range(2):
            s_ref[c] = lax.dot_general(k, qs[c], nt, preferred_element_type=F32)

    def accumulate(j, s_ref):
        vt = vt_ref[:, chunk(j)]
        for c in range(2):
            s = s_ref[c]
            m = m_ref[c]
            m_new = jnp.maximum(m, jnp.max(s, axis=0, keepdims=True))
            e = jnp.exp2(s - m_new).astype(BF16)
            m_ref[c] = m_new
            acc_ref[c] = jnp.exp2(m - m_new) * acc_ref[c] + jnp.dot(vt, e, preferred_element_type=F32)

    m_ref[...] = jnp.full(m_ref.shape, -jnp.inf, F32)
    acc_ref[...] = jnp.zeros(acc_ref.shape, F32)
    scores(0, sa_ref)

    def body(i, carry):
        j = 2 * i
        scores(j + 1, sb_ref)
        accumulate(j, sa_ref)
        scores(j + 2, sa_ref)
        accumulate(j + 1, sb_ref)
        return carry

    lax.fori_loop(0, n_chunks // 2 - 1, body, 0)
    scores(n_chunks - 1, sb_ref)
    accumulate(n_chunks - 2, sa_ref)
    accumulate(n_chunks - 1, sb_ref)
    lam = lam_ref[0]
    a1, a2 = acc_ref[0], acc_ref[1]
    ot = a1[:HEAD_W, :] / a1[HEAD_W:HEAD_W + 1, :] - lam * (a2[:HEAD_W, :] / a2[HEAD_W:HEAD_W + 1, :])
    ms = jnp.mean(ot * ot, axis=0, keepdims=True)
    ot = ot * lax.rsqrt(ms + EPS) * (g_ref[...] * out_scale)
    o_ref[...] = ot.T.astype(BF16)


def _attn(lam, aq, ak, avt, g, batch, seq, out_scale):
    t = aq.shape[0]
    nq = seq // ATTN_TQ
    return pl.pallas_call(
        functools.partial(_attn_kernel, seq=seq, out_scale=out_scale),
        grid=(batch, N_HEADS, nq),
        in_specs=[
            pl.BlockSpec(memory_space=pltpu.SMEM),
            pl.BlockSpec((ATTN_TQ, HEAD_W), lambda b, h, i: (b * nq + i, h)),
            pl.BlockSpec((seq, HEAD_W), lambda b, h, i: (b, h)),
            pl.BlockSpec((VT_ROWS, seq), lambda b, h, i: (h, b)),
            pl.BlockSpec((HEAD_W, 1), lambda b, h, i: (0, 0)),
        ],
        out_specs=pl.BlockSpec((ATTN_TQ, HEAD_W), lambda b, h, i: (b * nq + i, h)),
        out_shape=jax.ShapeDtypeStruct((t, ATTN_WIDTH), BF16),
        scratch_shapes=[
            pltpu.VMEM((2, ATTN_TK, ATTN_TQ), F32),
            pltpu.VMEM((2, ATTN_TK, ATTN_TQ), F32),
            pltpu.VMEM((2, 1, ATTN_TQ), F32),
            pltpu.VMEM((2, VT_ROWS, ATTN_TQ), F32),
        ],
        compiler_params=pltpu.CompilerParams(
            dimension_semantics=("arbitrary", "arbitrary", "arbitrary"), vmem_limit_bytes=VMEM_LIMIT),
        name="attn",
    )(lam, aq, ak, avt, g)


def _cumsum16(g, reverse):
    row = lax.broadcasted_iota(jnp.int32, g.shape, 0)
    b = g
    for sh in (1, 2, 4, 8):
        if reverse:
            b = b + jnp.where(row < HG_STEP - sh, pltpu.roll(b, HG_STEP - sh, axis=0), 0.0)
        else:
            b = b + jnp.where(row >= sh, pltpu.roll(b, sh, axis=0), 0.0)
    return b


def _hgrn_state(q, kk, b, v, st, reverse):
    qd = (q * jnp.exp2(b)).astype(BF16)
    o = lax.dot_general(qd, st.astype(BF16), (((1,), (1,)), ((), ())), preferred_element_type=F32)
    b_end = b[0:1, :] if reverse else b[HG_STEP - 1:HG_STEP, :]
    kd = (kk * jnp.exp2(b_end - b)).astype(BF16)
    upd = lax.dot_general(v, kd, (((0,), (0,)), ((), ())), preferred_element_type=F32)
    return o, st * jnp.exp2(b_end) + upd


def _hgrn_pairwise(q, kk, b, v, reverse):
    c = b - jnp.log2(kk)
    half = HG_STEP // 2
    row = lax.broadcasted_iota(jnp.int32, (half, HEAD_W), 0)
    vf = v.astype(F32)
    o = [jnp.zeros((half, HEAD_W), F32), jnp.zeros((half, HEAD_W), F32)]
    for s in range(HG_STEP):
        cs = c[s:s + 1, :]
        vs = vf[s:s + 1, :]
        for hi in range(2):
            r0 = hi * half
            if (r0 > s) if reverse else (r0 + half - 1 < s):
                continue
            d = b[r0:r0 + half, :] - cs
            if (r0 + half - 1 > s) if reverse else (r0 < s):
                seen = (row + r0 <= s) if reverse else (row + r0 >= s)
                d = jnp.where(seen, d, -jnp.inf)
            col = jnp.sum(q[r0:r0 + half, :] * jnp.exp2(d), axis=1, keepdims=True)
            o[hi] = o[hi] + col * vs
    return jnp.concatenate(o, axis=0)


def _hgrn_kernel(qf_ref, gf_ref, vf_ref, qb_ref, gb_ref, vb_ref, of_ref, ob_ref, sf_ref, sb_ref):
    @pl.when(pl.program_id(1) == 0)
    def _():
        sf_ref[...] = jnp.zeros_like(sf_ref)
        sb_ref[...] = jnp.zeros_like(sb_ref)

    nstep = HG_BLOCK // HG_STEP

    def body(i, carry):
        rf = pl.multiple_of(i * HG_STEP, HG_STEP)
        rb = pl.multiple_of((nstep - 1 - i) * HG_STEP, HG_STEP)
        chains = []
        for h in range(N_HEADS):
            cols = slice(h * HEAD_W, (h + 1) * HEAD_W)
            chains.append((qf_ref, gf_ref, vf_ref, of_ref, sf_ref, rf, h, cols, False))
            chains.append((qb_ref, gb_ref, vb_ref, ob_ref, sb_ref, rb, h, cols, True))
        live = []
        for q_ref, g_ref, v_ref, o_ref, s_ref, r, h, cols, reverse in chains:
            rows = pl.ds(r, HG_STEP)
            q, g, v = q_ref[rows, cols], g_ref[rows, cols], v_ref[rows, cols]
            kk = 1.0 - jnp.exp2(g)
            b = _cumsum16(g, reverse)
            o, st = _hgrn_state(q, kk, b, v, s_ref[h], reverse)
            o_ref[rows, cols] = o
            s_ref[h] = st
            live.append((q, kk, b, v))
        for (q_ref, g_ref, v_ref, o_ref, s_ref, r, h, cols, reverse), (q, kk, b, v) in zip(chains, live):
            rows = pl.ds(r, HG_STEP)
            o_ref[rows, cols] += _hgrn_pairwise(q, kk, b, v, reverse)
        return carry

    lax.fori_loop(0, nstep, body, 0)


def _hgrn(hq, gf, gb, hi, batch, seq):
    t = hq.shape[0]
    nb = seq // HG_BLOCK
    fwd = pl.BlockSpec((HG_BLOCK, HG_WIDTH), lambda b, j: (b * nb + j, 0))
    bwd = pl.BlockSpec((HG_BLOCK, HG_WIDTH), lambda b, j: (b * nb + nb - 1 - j, 0))
    return pl.pallas_call(
        _hgrn_kernel,
        grid=(batch, nb),
        in_specs=[fwd, fwd, fwd, bwd, bwd, bwd],
        out_specs=[fwd, bwd],
        out_shape=[jax.ShapeDtypeStruct((t, HG_WIDTH), F32)] * 2,
        scratch_shapes=[pltpu.VMEM((N_HEADS, HEAD_W, HEAD_W), F32)] * 2,
        compiler_params=pltpu.CompilerParams(
            dimension_semantics=("arbitrary", "arbitrary"), vmem_limit_bytes=VMEM_LIMIT),
        name="hgrn",
    )(hq, gf, hi, hq, gb, hi)


def _out_proj_kernel(x_ref, ao_ref, of_ref, ob_ref, gate_ref, hgg_ref, w_ref, g_ref, b_ref, o_ref):
    x = x_ref[...]
    ho = of_ref[...] + ob_ref[...]
    gate = gate_ref[...]
    hgg = hgg_ref[...]
    mix = jnp.dot(ao_ref[...], w_ref[0:ATTN_WIDTH, :], preferred_element_type=F32)
    for h in range(N_HEADS):
        cols = slice(h * HEAD_W, (h + 1) * HEAD_W)
        oh = ho[:, cols]
        ms = jnp.mean(oh * oh, axis=-1, keepdims=True)
        nh = (oh * lax.rsqrt(ms + EPS) * hgg * gate[:, cols]).astype(BF16)
        mix = mix + jnp.dot(nh, w_ref[ATTN_WIDTH + h * HEAD_W:ATTN_WIDTH + (h + 1) * HEAD_W, :],
                            preferred_element_type=F32)
    y = ALPHA * x + mix
    o_ref[...] = _layer_norm(y, g_ref[...], b_ref[...])


def _out_proj(x, ao, of, ob, gate, hgg, w_out, g, b):
    t = x.shape[0]
    row = lambda i: (i, 0)
    const = lambda i: (0, 0)
    grp = pl.BlockSpec((ROW_TILE, 512), row)
    return pl.pallas_call(
        _out_proj_kernel,
        grid=(t // ROW_TILE,),
        in_specs=[
            pl.BlockSpec((ROW_TILE, D_MODEL), row),
            grp, grp, grp, grp,
            pl.BlockSpec((1, HEAD_W), const),
            pl.BlockSpec((D_MODEL, D_MODEL), const, pipeline_mode=pl.Buffered(1)),
            pl.BlockSpec((1, D_MODEL), const),
            pl.BlockSpec((1, D_MODEL), const),
        ],
        out_specs=pl.BlockSpec((ROW_TILE, D_MODEL), row),
        out_shape=jax.ShapeDtypeStruct((t, D_MODEL), F32),
        compiler_params=pltpu.CompilerParams(
            dimension_semantics=("arbitrary",), vmem_limit_bytes=VMEM_LIMIT),
        name="out_proj",
    )(x, ao, of, ob, gate, hgg, w_out, g, b)


def _rope_tables(seq):
    half = ROT_DIM // 2
    inv_freq = jnp.float32(ROPE_THETA) ** (-jnp.arange(half, dtype=F32) * 2.0 / ROT_DIM)
    ang = jnp.arange(seq, dtype=F32)[:, None] * inv_freq[None, :]
    cos, sin = jnp.cos(ang), jnp.sin(ang)
    ones = jnp.ones((seq, ATTN_HEAD_DIM - ROT_DIM), F32)
    zeros = jnp.zeros((seq, ATTN_HEAD_DIM - ROT_DIM), F32)
    z8 = jnp.zeros((seq, half), F32)
    comp_c = jnp.concatenate([cos, cos, ones], axis=1)
    comp_a = jnp.concatenate([-sin, z8, zeros], axis=1)
    comp_b = jnp.concatenate([z8, sin, zeros], axis=1)
    two = lambda m: jnp.concatenate([m, m], axis=1)
    return two(comp_c), two(comp_a), two(comp_b)


def _trunk(x3, p):
    batch, seq, _ = x3.shape
    x = x3.reshape(batch * seq, D_MODEL)
    rope_c, rope_a, rope_b = _rope_tables(seq)
    for l in range(DEPTH):
        lam_init = 0.8 - 0.6 * math.exp(-0.3 * l)
        x = _ffn(x, p["wg"][l][0], p["wu"][l][0], p["wd"][l][0], p["ln_g"][l][0], p["ln_b"][l][0])
        aq, ak, avt, hq, gf, gb, hi, gate = _in_proj(
            x, p["w_in"][l], p["wvt"][l], rope_c, rope_a, rope_b, p["lbs"][0][l], p["lbs"][1][l], seq)
        ao = _attn(p["lam"][l], aq, ak, avt, p["attn_g"][l], batch, seq, 1.0 - lam_init)
        of, ob = _hgrn(hq, gf, gb, hi, batch, seq)
        x = _out_proj(x, ao, of, ob, gate, p["hg_g"][l], p["w_out"][l], p["ln_g"][l][1], p["ln_b"][l][1])
        x = _ffn(x, p["wg"][l][1], p["wu"][l][1], p["wd"][l][1], p["ln_g"][l][2], p["ln_b"][l][2])
    return x.reshape(batch, seq, D_MODEL)


def kernel(x_prompt, x_sample, w_in, w_out, attn_lambda, attn_norm_g, hg_norm_g, hg_lower_bound,
           ffn_w_gate, ffn_w_up, ffn_w_down, ln_g, ln_b):
    sm = jax.nn.softmax(hg_lower_bound.astype(F32), axis=1)
    lbs = jnp.maximum(jnp.cumsum(sm, axis=1) - sm[:, :1], 0.0)
    lp = attn_lambda.astype(F32)
    lam_init = jnp.asarray([0.8 - 0.6 * math.exp(-0.3 * l) for l in range(DEPTH)], F32)
    lam = (jnp.exp(jnp.sum(lp[:, 0] * lp[:, 1], axis=-1))
           - jnp.exp(jnp.sum(lp[:, 2] * lp[:, 3], axis=-1)) + lam_init)
    p = {
        "w_in": w_in.astype(BF16), "w_out": w_out.astype(BF16),
        "wvt": jnp.swapaxes(w_in[:, :, 2 * ATTN_WIDTH:3 * ATTN_WIDTH], 1, 2).astype(BF16),
        "wg": ffn_w_gate.astype(BF16), "wu": ffn_w_up.astype(BF16), "wd": ffn_w_down.astype(BF16),
        "ln_g": ln_g.reshape(DEPTH, 3, 1, D_MODEL), "ln_b": ln_b.reshape(DEPTH, 3, 1, D_MODEL),
        "attn_g": attn_norm_g.reshape(DEPTH, HEAD_W, 1), "hg_g": hg_norm_g.reshape(DEPTH, 1, HEAD_W),
        "lbs": lbs.reshape(2, DEPTH, 1, HG_WIDTH), "lam": lam.reshape(DEPTH, 1),
    }
    return (_trunk(x_prompt, p), _trunk(x_sample, p))
```

```python
import functools
import math

import jax
import jax.numpy as jnp
from jax import lax
from jax.experimental import pallas as pl
from jax.experimental.pallas import tpu as pltpu

D_MODEL = 1024
DEPTH = 2
ATTN_WIDTH = 512
HG_WIDTH = 512
HEAD_W = 128
N_HEADS = 4
ATTN_HEAD_DIM = 64
ROT_DIM = 16
ROPE_THETA = 500000.0
D_FF = 2816
ALPHA = (2 * DEPTH) ** 0.25
EPS = 1e-5
IN_WIDTH = 3 * ATTN_WIDTH + 5 * HG_WIDTH
LOG2E = math.log2(math.e)
ATTN_Q_SCALE = ATTN_HEAD_DIM ** -0.5 * LOG2E

ROW_TILE = 512
FF_CHUNK = 512
ATTN_TQ = 512
ATTN_TK = 512
VT_ROWS = HEAD_W + 16
HG_BLOCK = 512
HG_STEP = 16
VMEM_LIMIT = 56 * 1024 * 1024

BF16 = jnp.bfloat16
F32 = jnp.float32


def _layer_norm(y, g, b):
    mu = jnp.mean(y, axis=-1, keepdims=True)
    yc = y - mu
    var = jnp.mean(yc * yc, axis=-1, keepdims=True)
    return yc * lax.rsqrt(var + EPS) * g + b


def _sigmoid(z):
    return 1.0 / (1.0 + jnp.exp(-z))


def _ffn_block(x, wg_ref, wu_ref, wd_ref, g, b):
    xb = x.astype(BF16)
    acc = jnp.zeros((x.shape[0], D_MODEL), F32)
    for f0 in range(0, D_FF, FF_CHUNK):
        f1 = min(f0 + FF_CHUNK, D_FF)
        hg = jnp.dot(xb, wg_ref[:, f0:f1], preferred_element_type=F32)
        hu = jnp.dot(xb, wu_ref[:, f0:f1], preferred_element_type=F32)
        h = (hg * _sigmoid(hg) * hu).astype(BF16)
        acc = acc + jnp.dot(h, wd_ref[f0:f1, :], preferred_element_type=F32)
    return _layer_norm(ALPHA * x + 0.5 * acc, g, b)


def _ffn_kernel(x_ref, wg_ref, wu_ref, wd_ref, g_ref, b_ref, o_ref):
    o_ref[...] = _ffn_block(x_ref[...], wg_ref, wu_ref, wd_ref, g_ref[...], b_ref[...])


def _ffn(x, wg, wu, wd, g, b):
    t = x.shape[0]
    row = lambda i: (i, 0)
    const = lambda i: (0, 0)
    return pl.pallas_call(
        _ffn_kernel,
        grid=(t // ROW_TILE,),
        in_specs=[
            pl.BlockSpec((ROW_TILE, D_MODEL), row),
            pl.BlockSpec((D_MODEL, D_FF), const, pipeline_mode=pl.Buffered(1)),
            pl.BlockSpec((D_MODEL, D_FF), const, pipeline_mode=pl.Buffered(1)),
            pl.BlockSpec((D_FF, D_MODEL), const, pipeline_mode=pl.Buffered(1)),
            pl.BlockSpec((1, D_MODEL), const),
            pl.BlockSpec((1, D_MODEL), const),
        ],
        out_specs=pl.BlockSpec((ROW_TILE, D_MODEL), row),
        out_shape=jax.ShapeDtypeStruct((t, D_MODEL), F32),
        compiler_params=pltpu.CompilerParams(
            dimension_semantics=("arbitrary",), vmem_limit_bytes=VMEM_LIMIT),
        name="ffn",
    )(x, wg, wu, wd, g, b)


def _rope(x, c, a, bt):
    outs = []
    for h in range(N_HEADS):
        xh = x[:, h * HEAD_W:(h + 1) * HEAD_W]
        up = pltpu.roll(xh, HEAD_W - ROT_DIM // 2, axis=1)
        dn = pltpu.roll(xh, ROT_DIM // 2, axis=1)
        outs.append(xh * c + up * a + dn * bt)
    return outs


def _log_forget(z, lb):
    z2 = z * LOG2E
    log_sig = jnp.minimum(z2, 0.0) - jnp.log2(1.0 + jnp.exp2(-jnp.abs(z2)))
    a = jnp.log2(lb)
    c = jnp.log2(1.0 - lb) + log_sig
    log_f = jnp.maximum(a, c) + jnp.log2(1.0 + jnp.exp2(-jnp.abs(a - c)))
    return jnp.minimum(log_f, 0.0)


def _in_proj_kernel(x_ref, w_ref, wvt_ref, c_ref, a_ref, bt_ref, lbf_ref, lbb_ref,
                    aq_ref, ak_ref, avt_ref, hq_ref, gf_ref, gb_ref, hi_ref, gate_ref):
    xb = x_ref[...].astype(BF16)
    c, a, bt = c_ref[...], a_ref[...], bt_ref[...]

    def proj(j):
        return jnp.dot(xb, w_ref[:, j * 512:(j + 1) * 512], preferred_element_type=F32)

    q = _rope(proj(0), c, a, bt)
    for h in range(N_HEADS):
        aq_ref[:, h * HEAD_W:(h + 1) * HEAD_W] = (q[h] * ATTN_Q_SCALE).astype(BF16)
    k = _rope(proj(1), c, a, bt)
    for h in range(N_HEADS):
        ak_ref[:, h * HEAD_W:(h + 1) * HEAD_W] = k[h].astype(BF16)
    vt = lax.dot_general(wvt_ref[...], xb, (((1,), (1,)), ((), ())), preferred_element_type=F32).astype(BF16)
    ones = jnp.ones((VT_ROWS - HEAD_W, vt.shape[1]), BF16)
    for h in range(N_HEADS):
        avt_ref[h * VT_ROWS:h * VT_ROWS + HEAD_W, :] = vt[h * HEAD_W:(h + 1) * HEAD_W, :]
        avt_ref[h * VT_ROWS + HEAD_W:(h + 1) * VT_ROWS, :] = ones
    hq = proj(3)
    hq_ref[...] = hq * _sigmoid(hq)
    gf_ref[...] = _log_forget(proj(4), lbf_ref[...])
    gb_ref[...] = _log_forget(proj(5), lbb_ref[...])
    hi_ref[...] = proj(6).astype(BF16)
    gt = proj(7)
    gate_ref[...] = gt * _sigmoid(gt)


def _in_proj(x, w_in, wvt, rope_c, rope_a, rope_b, lb_f, lb_b, seq):
    t = x.shape[0]
    nseq = seq // ROW_TILE
    row = lambda i: (i, 0)
    const = lambda i: (0, 0)
    pos = lambda i: (i % nseq, 0)
    grp = pl.BlockSpec((ROW_TILE, 512), row)
    shp = lambda dt: jax.ShapeDtypeStruct((t, 512), dt)
    return pl.pallas_call(
        _in_proj_kernel,
        grid=(t // ROW_TILE,),
        in_specs=[
            pl.BlockSpec((ROW_TILE, D_MODEL), row),
            pl.BlockSpec((D_MODEL, IN_WIDTH), const, pipeline_mode=pl.Buffered(1)),
            pl.BlockSpec((ATTN_WIDTH, D_MODEL), const, pipeline_mode=pl.Buffered(1)),
            pl.BlockSpec((ROW_TILE, HEAD_W), pos),
            pl.BlockSpec((ROW_TILE, HEAD_W), pos),
            pl.BlockSpec((ROW_TILE, HEAD_W), pos),
            pl.BlockSpec((1, HG_WIDTH), const),
            pl.BlockSpec((1, HG_WIDTH), const),
        ],
        out_specs=[grp, grp, pl.BlockSpec((N_HEADS * VT_ROWS, ROW_TILE), lambda i: (0, i)),
                   grp, grp, grp, grp, grp],
        out_shape=[shp(BF16), shp(BF16), jax.ShapeDtypeStruct((N_HEADS * VT_ROWS, t), BF16),
                   shp(F32), shp(F32), shp(F32), shp(BF16), shp(F32)],
        compiler_params=pltpu.CompilerParams(
            dimension_semantics=("arbitrary",), vmem_limit_bytes=VMEM_LIMIT),
        name="in_proj",
    )(x, w_in, wvt, rope_c, rope_a, rope_b, lb_f, lb_b)


def _attn_kernel(lam_ref, q_ref, k_ref, vt_ref, g_ref, o_ref, sa_ref, sb_ref, m_ref, acc_ref, *, seq, out_scale):
    q = q_ref[...]
    lane = lax.broadcasted_iota(jnp.int32, q.shape, 1)
    zero = jnp.zeros_like(q)
    qs = (jnp.where(lane < ATTN_HEAD_DIM, q, zero), jnp.where(lane >= ATTN_HEAD_DIM, q, zero))
    nt = (((1,), (1,)), ((), ()))
    n_chunks = seq // ATTN_TK

    def chunk(j):
        return pl.ds(pl.multiple_of(j * ATTN_TK, ATTN_TK), ATTN_TK)

    def scores(j, s_ref):
        k = k_ref[chunk(j), :]
        for c in range(2):
            s_ref[c] = lax.dot_general(k, qs[c], nt, preferred_element_type=F32)

    def accumulate(j, s_ref):
        vt = vt_ref[:, chunk(j)]
        for c in range(2):
            s = s_ref[c]
            m = m_ref[c]
            m_new = jnp.maximum(m, jnp.max(s, axis=0, keepdims=True))
            e = jnp.exp2(s - m_new).astype(BF16)
            m_ref[c] = m_new
            acc_ref[c] = jnp.exp2(m - m_new) * acc_ref[c] + jnp.dot(vt, e, preferred_element_type=F32)

    m_ref[...] = jnp.full(m_ref.shape, -jnp.inf, F32)
    acc_ref[...] = jnp.zeros(acc_ref.shape, F32)
    scores(0, sa_ref)

    def body(i, carry):
        j = 2 * i
        scores(j + 1, sb_ref)
        accumulate(j, sa_ref)
        scores(j + 2, sa_ref)
        accumulate(j + 1, sb_ref)
        return carry

    lax.fori_loop(0, n_chunks // 2 - 1, body, 0)
    scores(n_chunks - 1, sb_ref)
    accumulate(n_chunks - 2, sa_ref)
    accumulate(n_chunks - 1, sb_ref)
    lam = lam_ref[0]
    a1, a2 = acc_ref[0], acc_ref[1]
    ot = a1[:HEAD_W, :] / a1[HEAD_W:HEAD_W + 1, :] - lam * (a2[:HEAD_W, :] / a2[HEAD_W:HEAD_W + 1, :])
    ms = jnp.mean(ot * ot, axis=0, keepdims=True)
    ot = ot * lax.rsqrt(ms + EPS) * (g_ref[...] * out_scale)
    o_ref[...] = ot.T.astype(BF16)


def _attn(lam, aq, ak, avt, g, batch, seq, out_scale):
    t = aq.shape[0]
    nq = seq // ATTN_TQ
    return pl.pallas_call(
        functools.partial(_attn_kernel, seq=seq, out_scale=out_scale),
        grid=(batch, N_HEADS, nq),
        in_specs=[
            pl.BlockSpec(memory_space=pltpu.SMEM),
            pl.BlockSpec((ATTN_TQ, HEAD_W), lambda b, h, i: (b * nq + i, h)),
            pl.BlockSpec((seq, HEAD_W), lambda b, h, i: (b, h)),
            pl.BlockSpec((VT_ROWS, seq), lambda b, h, i: (h, b)),
            pl.BlockSpec((HEAD_W, 1), lambda b, h, i: (0, 0)),
        ],
        out_specs=pl.BlockSpec((ATTN_TQ, HEAD_W), lambda b, h, i: (b * nq + i, h)),
        out_shape=jax.ShapeDtypeStruct((t, ATTN_WIDTH), BF16),
        scratch_shapes=[
            pltpu.VMEM((2, ATTN_TK, ATTN_TQ), F32),
            pltpu.VMEM((2, ATTN_TK, ATTN_TQ), F32),
            pltpu.VMEM((2, 1, ATTN_TQ), F32),
            pltpu.VMEM((2, VT_ROWS, ATTN_TQ), F32),
        ],
        compiler_params=pltpu.CompilerParams(
            dimension_semantics=("arbitrary", "arbitrary", "arbitrary"), vmem_limit_bytes=VMEM_LIMIT),
        name="attn",
    )(lam, aq, ak, avt, g)


def _cumsum16(g, reverse):
    row = lax.broadcasted_iota(jnp.int32, g.shape, 0)
    b = g
    for sh in (1, 2, 4, 8):
        if reverse:
            b = b + jnp.where(row < HG_STEP - sh, pltpu.roll(b, HG_STEP - sh, axis=0), 0.0)
        else:
            b = b + jnp.where(row >= sh, pltpu.roll(b, sh, axis=0), 0.0)
    return b


def _hgrn_state(q, kk, b, v, st, reverse):
    qd = (q * jnp.exp2(b)).astype(BF16)
    o = lax.dot_general(qd, st.astype(BF16), (((1,), (1,)), ((), ())), preferred_element_type=F32)
    b_end = b[0:1, :] if reverse else b[HG_STEP - 1:HG_STEP, :]
    kd = (kk * jnp.exp2(b_end - b)).astype(BF16)
    upd = lax.dot_general(v, kd, (((0,), (0,)), ((), ())), preferred_element_type=F32)
    return o, st * jnp.exp2(b_end) + upd


def _hgrn_pairwise(q, kk, b, v, reverse):
    c = b - jnp.log2(kk)
    half = HG_STEP // 2
    row = lax.broadcasted_iota(jnp.int32, (half, HEAD_W), 0)
    vf = v.astype(F32)
    o = [jnp.zeros((half, HEAD_W), F32), jnp.zeros((half, HEAD_W), F32)]
    for s in range(HG_STEP):
        cs = c[s:s + 1, :]
        vs = vf[s:s + 1, :]
        for hi in range(2):
            r0 = hi * half
            if (r0 > s) if reverse else (r0 + half - 1 < s):
                continue
            d = b[r0:r0 + half, :] - cs
            if (r0 + half - 1 > s) if reverse else (r0 < s):
                seen = (row + r0 <= s) if reverse else (row + r0 >= s)
                d = jnp.where(seen, d, -jnp.inf)
            col = jnp.sum(q[r0:r0 + half, :] * jnp.exp2(d), axis=1, keepdims=True)
            o[hi] = o[hi] + col * vs
    return jnp.concatenate(o, axis=0)


def _hgrn_kernel(qf_ref, gf_ref, vf_ref, qb_ref, gb_ref, vb_ref, of_ref, ob_ref, sf_ref, sb_ref):
    @pl.when(pl.program_id(1) == 0)
    def _():
        sf_ref[...] = jnp.zeros_like(sf_ref)
        sb_ref[...] = jnp.zeros_like(sb_ref)

    nstep = HG_BLOCK // HG_STEP

    def body(i, carry):
        rf = pl.multiple_of(i * HG_STEP, HG_STEP)
        rb = pl.multiple_of((nstep - 1 - i) * HG_STEP, HG_STEP)
        chains = []
        for h in range(N_HEADS):
            cols = slice(h * HEAD_W, (h + 1) * HEAD_W)
            chains.append((qf_ref, gf_ref, vf_ref, of_ref, sf_ref, rf, h, cols, False))
            chains.append((qb_ref, gb_ref, vb_ref, ob_ref, sb_ref, rb, h, cols, True))
        live = []
        for q_ref, g_ref, v_ref, o_ref, s_ref, r, h, cols, reverse in chains:
            rows = pl.ds(r, HG_STEP)
            q, g, v = q_ref[rows, cols], g_ref[rows, cols], v_ref[rows, cols]
            kk = 1.0 - jnp.exp2(g)
            b = _cumsum16(g, reverse)
            o, st = _hgrn_state(q, kk, b, v, s_ref[h], reverse)
            o_ref[rows, cols] = o
            s_ref[h] = st
            live.append((q, kk, b, v))
        for (q_ref, g_ref, v_ref, o_ref, s_ref, r, h, cols, reverse), (q, kk, b, v) in zip(chains, live):
            rows = pl.ds(r, HG_STEP)
            o_ref[rows, cols] += _hgrn_pairwise(q, kk, b, v, reverse)
        return carry

    lax.fori_loop(0, nstep, body, 0)


def _hgrn(hq, gf, gb, hi, batch, seq):
    t = hq.shape[0]
    nb = seq // HG_BLOCK
    fwd = pl.BlockSpec((HG_BLOCK, HG_WIDTH), lambda b, j: (b * nb + j, 0))
    bwd = pl.BlockSpec((HG_BLOCK, HG_WIDTH), lambda b, j: (b * nb + nb - 1 - j, 0))
    return pl.pallas_call(
        _hgrn_kernel,
        grid=(batch, nb),
        in_specs=[fwd, fwd, fwd, bwd, bwd, bwd],
        out_specs=[fwd, bwd],
        out_shape=[jax.ShapeDtypeStruct((t, HG_WIDTH), F32)] * 2,
        scratch_shapes=[pltpu.VMEM((N_HEADS, HEAD_W, HEAD_W), F32)] * 2,
        compiler_params=pltpu.CompilerParams(
            dimension_semantics=("arbitrary", "arbitrary"), vmem_limit_bytes=VMEM_LIMIT),
        name="hgrn",
    )(hq, gf, hi, hq, gb, hi)


def _out_proj_ffn_kernel(x_ref, ao_ref, of_ref, ob_ref, gate_ref, hgg_ref, w_ref, g1_ref, b1_ref,
                         wg_ref, wu_ref, wd_ref, g2_ref, b2_ref, o_ref):
    x = x_ref[...]
    ho = of_ref[...] + ob_ref[...]
    gate = gate_ref[...]
    hgg = hgg_ref[...]
    mix = jnp.dot(ao_ref[...], w_ref[0:ATTN_WIDTH, :], preferred_element_type=F32)
    for h in range(N_HEADS):
        cols = slice(h * HEAD_W, (h + 1) * HEAD_W)
        oh = ho[:, cols]
        ms = jnp.mean(oh * oh, axis=-1, keepdims=True)
        nh = (oh * lax.rsqrt(ms + EPS) * hgg * gate[:, cols]).astype(BF16)
        mix = mix + jnp.dot(nh, w_ref[ATTN_WIDTH + h * HEAD_W:ATTN_WIDTH + (h + 1) * HEAD_W, :],
                            preferred_element_type=F32)
    x = _layer_norm(ALPHA * x + mix, g1_ref[...], b1_ref[...])
    o_ref[...] = _ffn_block(x, wg_ref, wu_ref, wd_ref, g2_ref[...], b2_ref[...])


def _out_proj_ffn(x, ao, of, ob, gate, hgg, w_out, g1, b1, wg, wu, wd, g2, b2):
    t = x.shape[0]
    row = lambda i: (i, 0)
    const = lambda i: (0, 0)
    grp = pl.BlockSpec((ROW_TILE, 512), row)
    vec = pl.BlockSpec((1, D_MODEL), const)
    return pl.pallas_call(
        _out_proj_ffn_kernel,
        grid=(t // ROW_TILE,),
        in_specs=[
            pl.BlockSpec((ROW_TILE, D_MODEL), row),
            grp, grp, grp, grp,
            pl.BlockSpec((1, HEAD_W), const),
            pl.BlockSpec((D_MODEL, D_MODEL), const, pipeline_mode=pl.Buffered(1)),
            vec, vec,
            pl.BlockSpec((D_MODEL, D_FF), const, pipeline_mode=pl.Buffered(1)),
            pl.BlockSpec((D_MODEL, D_FF), const, pipeline_mode=pl.Buffered(1)),
            pl.BlockSpec((D_FF, D_MODEL), const, pipeline_mode=pl.Buffered(1)),
            vec, vec,
        ],
        out_specs=pl.BlockSpec((ROW_TILE, D_MODEL), row),
        out_shape=jax.ShapeDtypeStruct((t, D_MODEL), F32),
        compiler_params=pltpu.CompilerParams(
            dimension_semantics=("arbitrary",), vmem_limit_bytes=VMEM_LIMIT),
        name="out_proj_ffn",
    )(x, ao, of, ob, gate, hgg, w_out, g1, b1, wg, wu, wd, g2, b2)


def _rope_tables(seq):
    half = ROT_DIM // 2
    inv_freq = jnp.float32(ROPE_THETA) ** (-jnp.arange(half, dtype=F32) * 2.0 / ROT_DIM)
    ang = jnp.arange(seq, dtype=F32)[:, None] * inv_freq[None, :]
    cos, sin = jnp.cos(ang), jnp.sin(ang)
    ones = jnp.ones((seq, ATTN_HEAD_DIM - ROT_DIM), F32)
    zeros = jnp.zeros((seq, ATTN_HEAD_DIM - ROT_DIM), F32)
    z8 = jnp.zeros((seq, half), F32)
    comp_c = jnp.concatenate([cos, cos, ones], axis=1)
    comp_a = jnp.concatenate([-sin, z8, zeros], axis=1)
    comp_b = jnp.concatenate([z8, sin, zeros], axis=1)
    two = lambda m: jnp.concatenate([m, m], axis=1)
    return two(comp_c), two(comp_a), two(comp_b)


def _trunk(x3, p):
    batch, seq, _ = x3.shape
    x = x3.reshape(batch * seq, D_MODEL)
    rope_c, rope_a, rope_b = _rope_tables(seq)
    for l in range(DEPTH):
        lam_init = 0.8 - 0.6 * math.exp(-0.3 * l)
        x = _ffn(x, p["wg"][l][0], p["wu"][l][0], p["wd"][l][0], p["ln_g"][l][0], p["ln_b"][l][0])
        aq, ak, avt, hq, gf, gb, hi, gate = _in_proj(
            x, p["w_in"][l], p["wvt"][l], rope_c, rope_a, rope_b, p["lbs"][0][l], p["lbs"][1][l], seq)
        ao = _attn(p["lam"][l], aq, ak, avt, p["attn_g"][l], batch, seq, 1.0 - lam_init)
        of, ob = _hgrn(hq, gf, gb, hi, batch, seq)
        x = _out_proj_ffn(x, ao, of, ob, gate, p["hg_g"][l], p["w_out"][l], p["ln_g"][l][1], p["ln_b"][l][1],
                          p["wg"][l][1], p["wu"][l][1], p["wd"][l][1], p["ln_g"][l][2], p["ln_b"][l][2])
    return x.reshape(batch, seq, D_MODEL)


def kernel(x_prompt, x_sample, w_in, w_out, attn_lambda, attn_norm_g, hg_norm_g, hg_lower_bound,
           ffn_w_gate, ffn_w_up, ffn_w_down, ln_g, ln_b):
    sm = jax.nn.softmax(hg_lower_bound.astype(F32), axis=1)
    lbs = jnp.maximum(jnp.cumsum(sm, axis=1) - sm[:, :1], 0.0)
    lp = attn_lambda.astype(F32)
    lam_init = jnp.asarray([0.8 - 0.6 * math.exp(-0.3 * l) for l in range(DEPTH)], F32)
    lam = (jnp.exp(jnp.sum(lp[:, 0] * lp[:, 1], axis=-1))
           - jnp.exp(jnp.sum(lp[:, 2] * lp[:, 3], axis=-1)) + lam_init)
    p = {
        "w_in": w_in.astype(BF16), "w_out": w_out.astype(BF16),
        "wvt": jnp.swapaxes(w_in[:, :, 2 * ATTN_WIDTH:3 * ATTN_WIDTH], 1, 2).astype(BF16),
        "wg": ffn_w_gate.astype(BF16), "wu": ffn_w_up.astype(BF16), "wd": ffn_w_down.astype(BF16),
        "ln_g": ln_g.reshape(DEPTH, 3, 1, D_MODEL), "ln_b": ln_b.reshape(DEPTH, 3, 1, D_MODEL),
        "attn_g": attn_norm_g.reshape(DEPTH, HEAD_W, 1), "hg_g": hg_norm_g.reshape(DEPTH, 1, HEAD_W),
        "lbs": lbs.reshape(2, DEPTH, 1, HG_WIDTH), "lam": lam.reshape(DEPTH, 1),
    }
    return (_trunk(x_prompt, p), _trunk(x_sample, p))
```

```python
import functools
import math

import jax
import jax.numpy as jnp
from jax import lax
from jax.experimental import pallas as pl
from jax.experimental.pallas import tpu as pltpu

D_MODEL = 1024
DEPTH = 2
ATTN_WIDTH = 512
HG_WIDTH = 512
HEAD_W = 128
N_HEADS = 4
ATTN_HEAD_DIM = 64
ROT_DIM = 16
ROPE_THETA = 500000.0
D_FF = 2816
ALPHA = (2 * DEPTH) ** 0.25
EPS = 1e-5
IN_WIDTH = 3 * ATTN_WIDTH + 5 * HG_WIDTH
LOG2E = math.log2(math.e)
ATTN_Q_SCALE = ATTN_HEAD_DIM ** -0.5 * LOG2E

ROW_TILE = 512
FF_CHUNK = 512
ATTN_TQ = 512
ATTN_TK = 512
VT_ROWS = HEAD_W + 16
HG_BLOCK = 512
HG_STEP = 16
VMEM_LIMIT = 56 * 1024 * 1024

BF16 = jnp.bfloat16
F32 = jnp.float32
F8 = jnp.float8_e4m3fn
F8_BLOCK_MAX = 128.0
F8_MIN_BLOCK_MAX = 1e-30


def _layer_norm(y, g, b):
    mu = jnp.mean(y, axis=-1, keepdims=True)
    yc = y - mu
    var = jnp.mean(yc * yc, axis=-1, keepdims=True)
    return yc * lax.rsqrt(var + EPS) * g + b


def _sigmoid(z):
    return 1.0 / (1.0 + jnp.exp(-z))


def _ffn_block(x, wg_ref, wu_ref, wd_ref, g, b):
    xb = x.astype(BF16)
    acc = jnp.zeros((x.shape[0], D_MODEL), F32)
    for f0 in range(0, D_FF, FF_CHUNK):
        f1 = min(f0 + FF_CHUNK, D_FF)
        hg = jnp.dot(xb, wg_ref[:, f0:f1], preferred_element_type=F32)
        hu = jnp.dot(xb, wu_ref[:, f0:f1], preferred_element_type=F32)
        h = (hg * _sigmoid(hg) * hu).astype(BF16)
        acc = acc + jnp.dot(h, wd_ref[f0:f1, :], preferred_element_type=F32)
    return _layer_norm(ALPHA * x + 0.5 * acc, g, b)


def _ffn_kernel(x_ref, wg_ref, wu_ref, wd_ref, g_ref, b_ref, o_ref):
    o_ref[...] = _ffn_block(x_ref[...], wg_ref, wu_ref, wd_ref, g_ref[...], b_ref[...])


def _ffn(x, wg, wu, wd, g, b):
    t = x.shape[0]
    row = lambda i: (i, 0)
    const = lambda i: (0, 0)
    return pl.pallas_call(
        _ffn_kernel,
        grid=(t // ROW_TILE,),
        in_specs=[
            pl.BlockSpec((ROW_TILE, D_MODEL), row),
            pl.BlockSpec((D_MODEL, D_FF), const, pipeline_mode=pl.Buffered(1)),
            pl.BlockSpec((D_MODEL, D_FF), const, pipeline_mode=pl.Buffered(1)),
            pl.BlockSpec((D_FF, D_MODEL), const, pipeline_mode=pl.Buffered(1)),
            pl.BlockSpec((1, D_MODEL), const),
            pl.BlockSpec((1, D_MODEL), const),
        ],
        out_specs=pl.BlockSpec((ROW_TILE, D_MODEL), row),
        out_shape=jax.ShapeDtypeStruct((t, D_MODEL), F32),
        compiler_params=pltpu.CompilerParams(
            dimension_semantics=("arbitrary",), vmem_limit_bytes=VMEM_LIMIT),
        name="ffn",
    )(x, wg, wu, wd, g, b)


def _rope(x, c, a, bt):
    outs = []
    for h in range(N_HEADS):
        xh = x[:, h * HEAD_W:(h + 1) * HEAD_W]
        up = pltpu.roll(xh, HEAD_W - ROT_DIM // 2, axis=1)
        dn = pltpu.roll(xh, ROT_DIM // 2, axis=1)
        outs.append(xh * c + up * a + dn * bt)
    return outs


def _log_forget(z, lb):
    z2 = z * LOG2E
    log_sig = jnp.minimum(z2, 0.0) - jnp.log2(1.0 + jnp.exp2(-jnp.abs(z2)))
    a = jnp.log2(lb)
    c = jnp.log2(1.0 - lb) + log_sig
    log_f = jnp.maximum(a, c) + jnp.log2(1.0 + jnp.exp2(-jnp.abs(a - c)))
    return jnp.minimum(log_f, 0.0)


def _in_proj_kernel(x_ref, w_ref, wvt_ref, c_ref, a_ref, bt_ref, lbf_ref, lbb_ref,
                    aq_ref, ak_ref, avt_ref, hq_ref, gf_ref, gb_ref, hi_ref, gate_ref):
    xb = x_ref[...].astype(BF16)
    c, a, bt = c_ref[...], a_ref[...], bt_ref[...]

    def proj(j):
        return jnp.dot(xb, w_ref[:, j * 512:(j + 1) * 512], preferred_element_type=F32)

    q = _rope(proj(0), c, a, bt)
    for h in range(N_HEADS):
        aq_ref[:, h * HEAD_W:(h + 1) * HEAD_W] = (q[h] * ATTN_Q_SCALE).astype(BF16)
    k = _rope(proj(1), c, a, bt)
    for h in range(N_HEADS):
        ak_ref[:, h * HEAD_W:(h + 1) * HEAD_W] = k[h].astype(BF16)
    vt = lax.dot_general(wvt_ref[...], xb, (((1,), (1,)), ((), ())), preferred_element_type=F32).astype(BF16)
    ones = jnp.ones((VT_ROWS - HEAD_W, vt.shape[1]), BF16)
    for h in range(N_HEADS):
        avt_ref[h * VT_ROWS:h * VT_ROWS + HEAD_W, :] = vt[h * HEAD_W:(h + 1) * HEAD_W, :]
        avt_ref[h * VT_ROWS + HEAD_W:(h + 1) * VT_ROWS, :] = ones
    hq = proj(3)
    hq_ref[...] = hq * _sigmoid(hq)
    gf_ref[...] = _log_forget(proj(4), lbf_ref[...])
    gb_ref[...] = _log_forget(proj(5), lbb_ref[...])
    hi_ref[...] = proj(6).astype(BF16)
    gt = proj(7)
    gate_ref[...] = gt * _sigmoid(gt)


def _in_proj(x, w_in, wvt, rope_c, rope_a, rope_b, lb_f, lb_b, seq):
    t = x.shape[0]
    nseq = seq // ROW_TILE
    row = lambda i: (i, 0)
    const = lambda i: (0, 0)
    pos = lambda i: (i % nseq, 0)
    grp = pl.BlockSpec((ROW_TILE, 512), row)
    shp = lambda dt: jax.ShapeDtypeStruct((t, 512), dt)
    return pl.pallas_call(
        _in_proj_kernel,
        grid=(t // ROW_TILE,),
        in_specs=[
            pl.BlockSpec((ROW_TILE, D_MODEL), row),
            pl.BlockSpec((D_MODEL, IN_WIDTH), const, pipeline_mode=pl.Buffered(1)),
            pl.BlockSpec((ATTN_WIDTH, D_MODEL), const, pipeline_mode=pl.Buffered(1)),
            pl.BlockSpec((ROW_TILE, HEAD_W), pos),
            pl.BlockSpec((ROW_TILE, HEAD_W), pos),
            pl.BlockSpec((ROW_TILE, HEAD_W), pos),
            pl.BlockSpec((1, HG_WIDTH), const),
            pl.BlockSpec((1, HG_WIDTH), const),
        ],
        out_specs=[grp, grp, pl.BlockSpec((N_HEADS * VT_ROWS, ROW_TILE), lambda i: (0, i)),
                   grp, grp, grp, grp, grp],
        out_shape=[shp(BF16), shp(BF16), jax.ShapeDtypeStruct((N_HEADS * VT_ROWS, t), BF16),
                   shp(F32), shp(F32), shp(F32), shp(BF16), shp(F32)],
        compiler_params=pltpu.CompilerParams(
            dimension_semantics=("arbitrary",), vmem_limit_bytes=VMEM_LIMIT),
        name="in_proj",
    )(x, w_in, wvt, rope_c, rope_a, rope_b, lb_f, lb_b)


def _component_max(a):
    first = lax.broadcasted_iota(jnp.int32, a.shape, 1) < ATTN_HEAD_DIM
    m0 = jnp.max(jnp.where(first, a, 0.0), axis=1, keepdims=True)
    m1 = jnp.max(jnp.where(first, 0.0, a), axis=1, keepdims=True)
    return jnp.maximum(jnp.where(first, m0, m1), F8_MIN_BLOCK_MAX)


def _f8_features(x, keys):
    hi = x.astype(F8).astype(F32)
    lo = x - hi
    first = lax.broadcasted_iota(jnp.int32, x.shape, 1) < ATTN_HEAD_DIM
    hi_sw = pltpu.roll(hi, ATTN_HEAD_DIM, axis=1)
    lo_sw = pltpu.roll(lo, ATTN_HEAD_DIM, axis=1)
    zero = jnp.zeros_like(x)
    if keys:
        groups = ((jnp.where(first, hi, lo_sw), jnp.where(first, hi, zero)),
                  (jnp.where(first, hi_sw, lo), jnp.where(first, hi_sw, zero)))
    else:
        groups = ((jnp.where(first, hi, hi_sw), jnp.where(first, lo, zero)),
                  (jnp.where(first, hi_sw, hi), jnp.where(first, lo_sw, zero)))
    return [jnp.concatenate(g, axis=1).astype(F8) for g in groups]


def _attn_kernel(lam_ref, q_ref, k_ref, vt_ref, g_ref, o_ref, kf_ref, sa_ref, sb_ref, m_ref, acc_ref, *,
                 seq, out_scale):
    n_chunks = seq // ATTN_TK

    def chunk(j):
        return pl.ds(pl.multiple_of(j * ATTN_TK, ATTN_TK), ATTN_TK)

    def kmax_body(j, a):
        return jnp.maximum(a, jnp.max(jnp.abs(k_ref[chunk(j), :].astype(F32)), axis=0, keepdims=True))

    kmax = _component_max(lax.fori_loop(0, n_chunks, kmax_body, jnp.zeros((1, HEAD_W), F32)))

    def kfeat_body(j, carry):
        feats = _f8_features(k_ref[chunk(j), :].astype(F32) * (F8_BLOCK_MAX / kmax), keys=True)
        for c in range(2):
            kf_ref[c, chunk(j), :] = feats[c]
        return carry

    lax.fori_loop(0, n_chunks, kfeat_body, 0)

    def qblock(i, carry):
        rows = pl.ds(pl.multiple_of(i * ATTN_TQ, ATTN_TQ), ATTN_TQ)
        _attn_qblock(lam_ref, q_ref[rows, :], kmax, kf_ref, vt_ref, g_ref, o_ref.at[rows, :],
                     sa_ref, sb_ref, m_ref, acc_ref, seq=seq, out_scale=out_scale)
        return carry

    lax.fori_loop(0, seq // ATTN_TQ, qblock, 0)


def _attn_qblock(lam_ref, q, kmax, kf_ref, vt_ref, g_ref, o_ref, sa_ref, sb_ref, m_ref, acc_ref, *, seq, out_scale):
    qf32 = q.astype(F32)
    qmax = _component_max(jnp.max(jnp.abs(qf32), axis=0, keepdims=True))
    qf = _f8_features(qf32 * (F8_BLOCK_MAX / qmax), keys=False)
    unscale = kmax * qmax * (1.0 / (F8_BLOCK_MAX * F8_BLOCK_MAX))
    unscale = (unscale[:, 0:1], unscale[:, HEAD_W - 1:HEAD_W])
    nt = (((1,), (1,)), ((), ()))
    n_chunks = seq // ATTN_TK

    def chunk(j):
        return pl.ds(pl.multiple_of(j * ATTN_TK, ATTN_TK), ATTN_TK)

    def scores(j, s_ref):
        for c in range(2):
            s = lax.dot_general(kf_ref[c, chunk(j), :], qf[c], nt, preferred_element_type=F32)
            s_ref[c] = (s * unscale[c]).astype(BF16)

    def accumulate(j, s_ref):
        vt = vt_ref[:, chunk(j)]
        for c in range(2):
            s = s_ref[c]
            m = m_ref[c]
            m_new = jnp.maximum(m, jnp.max(s, axis=0, keepdims=True).astype(F32))
            e = jnp.exp2(s - m_new.astype(BF16))
            m_ref[c] = m_new
            acc_ref[c] = jnp.exp2(m - m_new) * acc_ref[c] + jnp.dot(vt, e, preferred_element_type=F32)

    m_ref[...] = jnp.full(m_ref.shape, -jnp.inf, F32)
    acc_ref[...] = jnp.zeros(acc_ref.shape, F32)
    scores(0, sa_ref)

    def body(i, carry):
        j = 2 * i
        scores(j + 1, sb_ref)
        accumulate(j, sa_ref)
        scores(j + 2, sa_ref)
        accumulate(j + 1, sb_ref)
        return carry

    lax.fori_loop(0, n_chunks // 2 - 1, body, 0)
    scores(n_chunks - 1, sb_ref)
    accumulate(n_chunks - 2, sa_ref)
    accumulate(n_chunks - 1, sb_ref)
    lam = lam_ref[0]
    a1, a2 = acc_ref[0], acc_ref[1]
    ot = a1[:HEAD_W, :] / a1[HEAD_W:HEAD_W + 1, :] - lam * (a2[:HEAD_W, :] / a2[HEAD_W:HEAD_W + 1, :])
    ms = jnp.mean(ot * ot, axis=0, keepdims=True)
    ot = ot * lax.rsqrt(ms + EPS) * (g_ref[...] * out_scale)
    o_ref[...] = ot.T.astype(BF16)


def _attn(lam, aq, ak, avt, g, batch, seq, out_scale):
    t = aq.shape[0]
    return pl.pallas_call(
        functools.partial(_attn_kernel, seq=seq, out_scale=out_scale),
        grid=(batch, N_HEADS),
        in_specs=[
            pl.BlockSpec(memory_space=pltpu.SMEM),
            pl.BlockSpec((seq, HEAD_W), lambda b, h: (b, h)),
            pl.BlockSpec((seq, HEAD_W), lambda b, h: (b, h)),
            pl.BlockSpec((VT_ROWS, seq), lambda b, h: (h, b)),
            pl.BlockSpec((HEAD_W, 1), lambda b, h: (0, 0)),
        ],
        out_specs=pl.BlockSpec((seq, HEAD_W), lambda b, h: (b, h)),
        out_shape=jax.ShapeDtypeStruct((t, ATTN_WIDTH), BF16),
        scratch_shapes=[
            pltpu.VMEM((2, seq, 2 * HEAD_W), F8),
            pltpu.VMEM((2, ATTN_TK, ATTN_TQ), BF16),
            pltpu.VMEM((2, ATTN_TK, ATTN_TQ), BF16),
            pltpu.VMEM((2, 1, ATTN_TQ), F32),
            pltpu.VMEM((2, VT_ROWS, ATTN_TQ), F32),
        ],
        compiler_params=pltpu.CompilerParams(
            dimension_semantics=("arbitrary", "arbitrary"), vmem_limit_bytes=VMEM_LIMIT),
        name="attn",
    )(lam, aq, ak, avt, g)


def _cumsum16(g, reverse):
    row = lax.broadcasted_iota(jnp.int32, g.shape, 0)
    b = g
    for sh in (1, 2, 4, 8):
        if reverse:
            b = b + jnp.where(row < HG_STEP - sh, pltpu.roll(b, HG_STEP - sh, axis=0), 0.0)
        else:
            b = b + jnp.where(row >= sh, pltpu.roll(b, sh, axis=0), 0.0)
    return b


def _hgrn_state(q, kk, b, v, st, reverse):
    qd = (q * jnp.exp2(b)).astype(BF16)
    o = lax.dot_general(qd, st.astype(BF16), (((1,), (1,)), ((), ())), preferred_element_type=F32)
    b_end = b[0:1, :] if reverse else b[HG_STEP - 1:HG_STEP, :]
    kd = (kk * jnp.exp2(b_end - b)).astype(BF16)
    upd = lax.dot_general(v, kd, (((0,), (0,)), ((), ())), preferred_element_type=F32)
    return o, st * jnp.exp2(b_end) + upd


def _hgrn_pairwise(q, kk, b, v, reverse):
    c = b - jnp.log2(kk)
    half = HG_STEP // 2
    row = lax.broadcasted_iota(jnp.int32, (half, HEAD_W), 0)
    vf = v.astype(F32)
    o = [jnp.zeros((half, HEAD_W), F32), jnp.zeros((half, HEAD_W), F32)]
    for s in range(HG_STEP):
        cs = c[s:s + 1, :]
        vs = vf[s:s + 1, :]
        for hi in range(2):
            r0 = hi * half
            if (r0 > s) if reverse else (r0 + half - 1 < s):
                continue
            d = b[r0:r0 + half, :] - cs
            if (r0 + half - 1 > s) if reverse else (r0 < s):
                seen = (row + r0 <= s) if reverse else (row + r0 >= s)
                d = jnp.where(seen, d, -jnp.inf)
            col = jnp.sum(q[r0:r0 + half, :] * jnp.exp2(d), axis=1, keepdims=True)
            o[hi] = o[hi] + col * vs
    return jnp.concatenate(o, axis=0)


def _hgrn_kernel(qf_ref, gf_ref, vf_ref, qb_ref, gb_ref, vb_ref, of_ref, ob_ref, sf_ref, sb_ref):
    @pl.when(pl.program_id(1) == 0)
    def _():
        sf_ref[...] = jnp.zeros_like(sf_ref)
        sb_ref[...] = jnp.zeros_like(sb_ref)

    nstep = HG_BLOCK // HG_STEP

    def body(i, carry):
        rf = pl.multiple_of(i * HG_STEP, HG_STEP)
        rb = pl.multiple_of((nstep - 1 - i) * HG_STEP, HG_STEP)
        chains = []
        for h in range(N_HEADS):
            cols = slice(h * HEAD_W, (h + 1) * HEAD_W)
            chains.append((qf_ref, gf_ref, vf_ref, of_ref, sf_ref, rf, h, cols, False))
            chains.append((qb_ref, gb_ref, vb_ref, ob_ref, sb_ref, rb, h, cols, True))
        live = []
        for q_ref, g_ref, v_ref, o_ref, s_ref, r, h, cols, reverse in chains:
            rows = pl.ds(r, HG_STEP)
            q, g, v = q_ref[rows, cols], g_ref[rows, cols], v_ref[rows, cols]
            kk = 1.0 - jnp.exp2(g)
            b = _cumsum16(g, reverse)
            o, st = _hgrn_state(q, kk, b, v, s_ref[h], reverse)
            o_ref[rows, cols] = o
            s_ref[h] = st
            live.append((q, kk, b, v))
        for (q_ref, g_ref, v_ref, o_ref, s_ref, r, h, cols, reverse), (q, kk, b, v) in zip(chains, live):
            rows = pl.ds(r, HG_STEP)
            o_ref[rows, cols] += _hgrn_pairwise(q, kk, b, v, reverse)
        return carry

    lax.fori_loop(0, nstep, body, 0)


def _hgrn(hq, gf, gb, hi, batch, seq):
    t = hq.shape[0]
    nb = seq // HG_BLOCK
    fwd = pl.BlockSpec((HG_BLOCK, HG_WIDTH), lambda b, j: (b * nb + j, 0))
    bwd = pl.BlockSpec((HG_BLOCK, HG_WIDTH), lambda b, j: (b * nb + nb - 1 - j, 0))
    return pl.pallas_call(
        _hgrn_kernel,
        grid=(batch, nb),
        in_specs=[fwd, fwd, fwd, bwd, bwd, bwd],
        out_specs=[fwd, bwd],
        out_shape=[jax.ShapeDtypeStruct((t, HG_WIDTH), F32)] * 2,
        scratch_shapes=[pltpu.VMEM((N_HEADS, HEAD_W, HEAD_W), F32)] * 2,
        compiler_params=pltpu.CompilerParams(
            dimension_semantics=("arbitrary", "arbitrary"), vmem_limit_bytes=VMEM_LIMIT),
        name="hgrn",
    )(hq, gf, hi, hq, gb, hi)


def _out_proj_ffn_kernel(x_ref, ao_ref, of_ref, ob_ref, gate_ref, hgg_ref, w_ref, g1_ref, b1_ref,
                         wg_ref, wu_ref, wd_ref, g2_ref, b2_ref, o_ref):
    x = x_ref[...]
    ho = of_ref[...] + ob_ref[...]
    gate = gate_ref[...]
    hgg = hgg_ref[...]
    mix = jnp.dot(ao_ref[...], w_ref[0:ATTN_WIDTH, :], preferred_element_type=F32)
    for h in range(N_HEADS):
        cols = slice(h * HEAD_W, (h + 1) * HEAD_W)
        oh = ho[:, cols]
        ms = jnp.mean(oh * oh, axis=-1, keepdims=True)
        nh = (oh * lax.rsqrt(ms + EPS) * hgg * gate[:, cols]).astype(BF16)
        mix = mix + jnp.dot(nh, w_ref[ATTN_WIDTH + h * HEAD_W:ATTN_WIDTH + (h + 1) * HEAD_W, :],
                            preferred_element_type=F32)
    x = _layer_norm(ALPHA * x + mix, g1_ref[...], b1_ref[...])
    o_ref[...] = _ffn_block(x, wg_ref, wu_ref, wd_ref, g2_ref[...], b2_ref[...])


def _out_proj_ffn(x, ao, of, ob, gate, hgg, w_out, g1, b1, wg, wu, wd, g2, b2):
    t = x.shape[0]
    row = lambda i: (i, 0)
    const = lambda i: (0, 0)
    grp = pl.BlockSpec((ROW_TILE, 512), row)
    vec = pl.BlockSpec((1, D_MODEL), const)
    return pl.pallas_call(
        _out_proj_ffn_kernel,
        grid=(t // ROW_TILE,),
        in_specs=[
            pl.BlockSpec((ROW_TILE, D_MODEL), row),
            grp, grp, grp, grp,
            pl.BlockSpec((1, HEAD_W), const),
            pl.BlockSpec((D_MODEL, D_MODEL), const, pipeline_mode=pl.Buffered(1)),
            vec, vec,
            pl.BlockSpec((D_MODEL, D_FF), const, pipeline_mode=pl.Buffered(1)),
            pl.BlockSpec((D_MODEL, D_FF), const, pipeline_mode=pl.Buffered(1)),
            pl.BlockSpec((D_FF, D_MODEL), const, pipeline_mode=pl.Buffered(1)),
            vec, vec,
        ],
        out_specs=pl.BlockSpec((ROW_TILE, D_MODEL), row),
        out_shape=jax.ShapeDtypeStruct((t, D_MODEL), F32),
        compiler_params=pltpu.CompilerParams(
            dimension_semantics=("arbitrary",), vmem_limit_bytes=VMEM_LIMIT),
        name="out_proj_ffn",
    )(x, ao, of, ob, gate, hgg, w_out, g1, b1, wg, wu, wd, g2, b2)


def _rope_tables(seq):
    half = ROT_DIM // 2
    inv_freq = jnp.float32(ROPE_THETA) ** (-jnp.arange(half, dtype=F32) * 2.0 / ROT_DIM)
    ang = jnp.arange(seq, dtype=F32)[:, None] * inv_freq[None, :]
    cos, sin = jnp.cos(ang), jnp.sin(ang)
    ones = jnp.ones((seq, ATTN_HEAD_DIM - ROT_DIM), F32)
    zeros = jnp.zeros((seq, ATTN_HEAD_DIM - ROT_DIM), F32)
    z8 = jnp.zeros((seq, half), F32)
    comp_c = jnp.concatenate([cos, cos, ones], axis=1)
    comp_a = jnp.concatenate([-sin, z8, zeros], axis=1)
    comp_b = jnp.concatenate([z8, sin, zeros], axis=1)
    two = lambda m: jnp.concatenate([m, m], axis=1)
    return two(comp_c), two(comp_a), two(comp_b)


def _trunk(x3, p):
    batch, seq, _ = x3.shape
    x = x3.reshape(batch * seq, D_MODEL)
    rope_c, rope_a, rope_b = _rope_tables(seq)
    for l in range(DEPTH):
        lam_init = 0.8 - 0.6 * math.exp(-0.3 * l)
        x = _ffn(x, p["wg"][l][0], p["wu"][l][0], p["wd"][l][0], p["ln_g"][l][0], p["ln_b"][l][0])
        aq, ak, avt, hq, gf, gb, hi, gate = _in_proj(
            x, p["w_in"][l], p["wvt"][l], rope_c, rope_a, rope_b, p["lbs"][0][l], p["lbs"][1][l], seq)
        ao = _attn(p["lam"][l], aq, ak, avt, p["attn_g"][l], batch, seq, 1.0 - lam_init)
        of, ob = _hgrn(hq, gf, gb, hi, batch, seq)
        x = _out_proj_ffn(x, ao, of, ob, gate, p["hg_g"][l], p["w_out"][l], p["ln_g"][l][1], p["ln_b"][l][1],
                          p["wg"][l][1], p["wu"][l][1], p["wd"][l][1], p["ln_g"][l][2], p["ln_b"][l][2])
    return x.reshape(batch, seq, D_MODEL)


def kernel(x_prompt, x_sample, w_in, w_out, attn_lambda, attn_norm_g, hg_norm_g, hg_lower_bound,
           ffn_w_gate, ffn_w_up, ffn_w_down, ln_g, ln_b):
    sm = jax.nn.softmax(hg_lower_bound.astype(F32), axis=1)
    lbs = jnp.maximum(jnp.cumsum(sm, axis=1) - sm[:, :1], 0.0)
    lp = attn_lambda.astype(F32)
    lam_init = jnp.asarray([0.8 - 0.6 * math.exp(-0.3 * l) for l in range(DEPTH)], F32)
    lam = (jnp.exp(jnp.sum(lp[:, 0] * lp[:, 1], axis=-1))
           - jnp.exp(jnp.sum(lp[:, 2] * lp[:, 3], axis=-1)) + lam_init)
    p = {
        "w_in": w_in.astype(BF16), "w_out": w_out.astype(BF16),
        "wvt": jnp.swapaxes(w_in[:, :, 2 * ATTN_WIDTH:3 * ATTN_WIDTH], 1, 2).astype(BF16),
        "wg": ffn_w_gate.astype(BF16), "wu": ffn_w_up.astype(BF16), "wd": ffn_w_down.astype(BF16),
        "ln_g": ln_g.reshape(DEPTH, 3, 1, D_MODEL), "ln_b": ln_b.reshape(DEPTH, 3, 1, D_MODEL),
        "attn_g": attn_norm_g.reshape(DEPTH, HEAD_W, 1), "hg_g": hg_norm_g.reshape(DEPTH, 1, HEAD_W),
        "lbs": lbs.reshape(2, DEPTH, 1, HG_WIDTH), "lam": lam.reshape(DEPTH, 1),
    }
    return (_trunk(x_prompt, p), _trunk(x_sample, p))
```

```python
import functools
import math

import jax
import jax.numpy as jnp
from jax import lax
from jax.experimental import pallas as pl
from jax.experimental.pallas import tpu as pltpu

D_MODEL = 1024
DEPTH = 2
ATTN_WIDTH = 512
HG_WIDTH = 512
HEAD_W = 128
N_HEADS = 4
ATTN_HEAD_DIM = 64
ROT_DIM = 16
ROPE_THETA = 500000.0
D_FF = 2816
ALPHA = (2 * DEPTH) ** 0.25
EPS = 1e-5
IN_WIDTH = 3 * ATTN_WIDTH + 5 * HG_WIDTH
LOG2E = math.log2(math.e)
ATTN_Q_SCALE = ATTN_HEAD_DIM ** -0.5 * LOG2E

ROW_TILE = 512
FFN_ROW_TILE = 2 * ROW_TILE
FF_CHUNK = 512
ATTN_TQ = 512
ATTN_UNROLL = 4
ATTN_TK = 512
VT_ROWS = HEAD_W + 16
HG_BLOCK = 512
HG_STEP = 16
VMEM_LIMIT = 56 * 1024 * 1024

BF16 = jnp.bfloat16
F32 = jnp.float32
F8 = jnp.float8_e4m3fn
F8_BLOCK_MAX = 128.0
F8_MIN_BLOCK_MAX = 1e-30


def _layer_norm(y, g, b):
    mu = jnp.mean(y, axis=-1, keepdims=True)
    yc = y - mu
    var = jnp.mean(yc * yc, axis=-1, keepdims=True)
    return yc * lax.rsqrt(var + EPS) * g + b


def _sigmoid(z):
    return 1.0 / (1.0 + jnp.exp(-z))


def _ffn_block(x, wg_ref, wu_ref, wd_ref, g, b):
    xb = x.astype(BF16)
    acc = jnp.zeros((x.shape[0], D_MODEL), F32)
    for f0 in range(0, D_FF, FF_CHUNK):
        f1 = min(f0 + FF_CHUNK, D_FF)
        hg = jnp.dot(xb, wg_ref[:, f0:f1], preferred_element_type=F32)
        hu = jnp.dot(xb, wu_ref[:, f0:f1], preferred_element_type=F32)
        h = (hg * _sigmoid(hg) * hu).astype(BF16)
        acc = acc + jnp.dot(h, wd_ref[f0:f1, :], preferred_element_type=F32)
    return _layer_norm(ALPHA * x + 0.5 * acc, g, b)


def _ffn_kernel(x_ref, wg_ref, wu_ref, wd_ref, g_ref, b_ref, o_ref):
    for r in range(0, FFN_ROW_TILE, ROW_TILE):
        o_ref[r:r + ROW_TILE, :] = _ffn_block(x_ref[r:r + ROW_TILE, :], wg_ref, wu_ref, wd_ref,
                                              g_ref[...], b_ref[...])


def _ffn(x, wg, wu, wd, g, b):
    t = x.shape[0]
    row = lambda i: (i, 0)
    const = lambda i: (0, 0)
    return pl.pallas_call(
        _ffn_kernel,
        grid=(t // FFN_ROW_TILE,),
        in_specs=[
            pl.BlockSpec((FFN_ROW_TILE, D_MODEL), row),
            pl.BlockSpec((D_MODEL, D_FF), const, pipeline_mode=pl.Buffered(1)),
            pl.BlockSpec((D_MODEL, D_FF), const, pipeline_mode=pl.Buffered(1)),
            pl.BlockSpec((D_FF, D_MODEL), const, pipeline_mode=pl.Buffered(1)),
            pl.BlockSpec((1, D_MODEL), const),
            pl.BlockSpec((1, D_MODEL), const),
        ],
        out_specs=pl.BlockSpec((FFN_ROW_TILE, D_MODEL), row),
        out_shape=jax.ShapeDtypeStruct((t, D_MODEL), F32),
        compiler_params=pltpu.CompilerParams(
            dimension_semantics=("arbitrary",), vmem_limit_bytes=VMEM_LIMIT),
        name="ffn",
    )(x, wg, wu, wd, g, b)


def _rope(x, c, a, bt):
    outs = []
    for h in range(N_HEADS):
        xh = x[:, h * HEAD_W:(h + 1) * HEAD_W]
        up = pltpu.roll(xh, HEAD_W - ROT_DIM // 2, axis=1)
        dn = pltpu.roll(xh, ROT_DIM // 2, axis=1)
        outs.append(xh * c + up * a + dn * bt)
    return outs


def _log_forget(z, lb):
    z2 = z * LOG2E
    log_sig = jnp.minimum(z2, 0.0) - jnp.log2(1.0 + jnp.exp2(-jnp.abs(z2)))
    a = jnp.log2(lb)
    c = jnp.log2(1.0 - lb) + log_sig
    log_f = jnp.maximum(a, c) + jnp.log2(1.0 + jnp.exp2(-jnp.abs(a - c)))
    return jnp.minimum(log_f, 0.0)


def _in_proj_kernel(x_ref, w_ref, wvt_ref, c_ref, a_ref, bt_ref, lbf_ref, lbb_ref,
                    aq_ref, ak_ref, avt_ref, hq_ref, gf_ref, gb_ref, hi_ref, gate_ref):
    xb = x_ref[...].astype(BF16)
    c, a, bt = c_ref[...], a_ref[...], bt_ref[...]

    def proj(j):
        return jnp.dot(xb, w_ref[:, j * 512:(j + 1) * 512], preferred_element_type=F32)

    gf_ref[...] = _log_forget(proj(4), lbf_ref[...])
    gb_ref[...] = _log_forget(proj(5), lbb_ref[...])
    q = _rope(proj(0), c, a, bt)
    for h in range(N_HEADS):
        aq_ref[:, h * HEAD_W:(h + 1) * HEAD_W] = (q[h] * ATTN_Q_SCALE).astype(BF16)
    k = _rope(proj(1), c, a, bt)
    for h in range(N_HEADS):
        ak_ref[:, h * HEAD_W:(h + 1) * HEAD_W] = k[h].astype(BF16)
    hq = proj(3)
    hq_ref[...] = hq * _sigmoid(hq)
    gt = proj(7)
    gate_ref[...] = gt * _sigmoid(gt)
    vt = lax.dot_general(wvt_ref[...], xb, (((1,), (1,)), ((), ())), preferred_element_type=F32).astype(BF16)
    ones = jnp.ones((VT_ROWS - HEAD_W, vt.shape[1]), BF16)
    for h in range(N_HEADS):
        avt_ref[h * VT_ROWS:h * VT_ROWS + HEAD_W, :] = vt[h * HEAD_W:(h + 1) * HEAD_W, :]
        avt_ref[h * VT_ROWS + HEAD_W:(h + 1) * VT_ROWS, :] = ones
    hi_ref[...] = proj(6).astype(BF16)


def _in_proj(x, w_in, wvt, rope_c, rope_a, rope_b, lb_f, lb_b, seq):
    t = x.shape[0]
    nseq = seq // ROW_TILE
    row = lambda i: (i, 0)
    const = lambda i: (0, 0)
    pos = lambda i: (i % nseq, 0)
    grp = pl.BlockSpec((ROW_TILE, 512), row)
    shp = lambda dt: jax.ShapeDtypeStruct((t, 512), dt)
    return pl.pallas_call(
        _in_proj_kernel,
        grid=(t // ROW_TILE,),
        in_specs=[
            pl.BlockSpec((ROW_TILE, D_MODEL), row),
            pl.BlockSpec((D_MODEL, IN_WIDTH), const, pipeline_mode=pl.Buffered(1)),
            pl.BlockSpec((ATTN_WIDTH, D_MODEL), const, pipeline_mode=pl.Buffered(1)),
            pl.BlockSpec((ROW_TILE, HEAD_W), pos),
            pl.BlockSpec((ROW_TILE, HEAD_W), pos),
            pl.BlockSpec((ROW_TILE, HEAD_W), pos),
            pl.BlockSpec((1, HG_WIDTH), const),
            pl.BlockSpec((1, HG_WIDTH), const),
        ],
        out_specs=[grp, grp, pl.BlockSpec((N_HEADS * VT_ROWS, ROW_TILE), lambda i: (0, i)),
                   grp, grp, grp, grp, grp],
        out_shape=[shp(BF16), shp(BF16), jax.ShapeDtypeStruct((N_HEADS * VT_ROWS, t), BF16),
                   shp(F32), shp(F32), shp(F32), shp(BF16), shp(F32)],
        compiler_params=pltpu.CompilerParams(
            dimension_semantics=("arbitrary",), vmem_limit_bytes=VMEM_LIMIT),
        name="in_proj",
    )(x, w_in, wvt, rope_c, rope_a, rope_b, lb_f, lb_b)


def _component_max(a):
    first = lax.broadcasted_iota(jnp.int32, a.shape, 1) < ATTN_HEAD_DIM
    m0 = jnp.max(jnp.where(first, a, 0.0), axis=1, keepdims=True)
    m1 = jnp.max(jnp.where(first, 0.0, a), axis=1, keepdims=True)
    return jnp.maximum(jnp.where(first, m0, m1), F8_MIN_BLOCK_MAX)


def _f8_features(x, keys):
    hi = x.astype(F8).astype(F32)
    lo = x - hi
    first = lax.broadcasted_iota(jnp.int32, x.shape, 1) < ATTN_HEAD_DIM
    hi_sw = pltpu.roll(hi, ATTN_HEAD_DIM, axis=1)
    lo_sw = pltpu.roll(lo, ATTN_HEAD_DIM, axis=1)
    zero = jnp.zeros_like(x)
    if keys:
        groups = ((jnp.where(first, hi, lo_sw), jnp.where(first, hi, zero)),
                  (jnp.where(first, hi_sw, lo), jnp.where(first, hi_sw, zero)))
    else:
        groups = ((jnp.where(first, hi, hi_sw), jnp.where(first, lo, zero)),
                  (jnp.where(first, hi_sw, hi), jnp.where(first, lo_sw, zero)))
    return [jnp.concatenate(g, axis=1).astype(F8) for g in groups]


def _attn_kernel(lam_ref, q_ref, k_ref, vt_ref, g_ref, o_ref, kf_ref, sa_ref, sb_ref, m_ref, acc_ref, *,
                 seq, out_scale):
    n_chunks = seq // ATTN_TK

    def chunk(j):
        return pl.ds(pl.multiple_of(j * ATTN_TK, ATTN_TK), ATTN_TK)

    def kmax_body(j, a):
        return jnp.maximum(a, jnp.max(jnp.abs(k_ref[chunk(j), :].astype(F32)), axis=0, keepdims=True))

    kmax = _component_max(lax.fori_loop(0, n_chunks, kmax_body, jnp.zeros((1, HEAD_W), F32)))

    def kfeat_body(j, carry):
        feats = _f8_features(k_ref[chunk(j), :].astype(F32) * (F8_BLOCK_MAX / kmax), keys=True)
        for c in range(2):
            kf_ref[c, chunk(j), :] = feats[c]
        return carry

    lax.fori_loop(0, n_chunks, kfeat_body, 0)

    def qblock(i, carry):
        rows = pl.ds(pl.multiple_of(i * ATTN_TQ, ATTN_TQ), ATTN_TQ)
        _attn_qblock(lam_ref, q_ref[rows, :], kmax, kf_ref, vt_ref, g_ref, o_ref.at[rows, :],
                     sa_ref, sb_ref, m_ref, acc_ref, seq=seq, out_scale=out_scale)
        return carry

    lax.fori_loop(0, seq // ATTN_TQ, qblock, 0)


def _attn_qblock(lam_ref, q, kmax, kf_ref, vt_ref, g_ref, o_ref, sa_ref, sb_ref, m_ref, acc_ref, *, seq, out_scale):
    qf32 = q.astype(F32)
    qmax = _component_max(jnp.max(jnp.abs(qf32), axis=0, keepdims=True))
    qf = _f8_features(qf32 * (F8_BLOCK_MAX / qmax), keys=False)
    unscale = kmax * qmax * (1.0 / (F8_BLOCK_MAX * F8_BLOCK_MAX))
    unscale = (unscale[:, 0:1], unscale[:, HEAD_W - 1:HEAD_W])
    nt = (((1,), (1,)), ((), ()))
    n_chunks = seq // ATTN_TK

    def chunk(j):
        return pl.ds(pl.multiple_of(j * ATTN_TK, ATTN_TK), ATTN_TK)

    def scores(j, s_ref):
        for c in range(2):
            s = lax.dot_general(kf_ref[c, chunk(j), :], qf[c], nt, preferred_element_type=F32)
            s_ref[c] = (s * unscale[c]).astype(BF16)

    def accumulate(j, s_ref):
        vt = vt_ref[:, chunk(j)]
        for c in range(2):
            s = s_ref[c]
            m = m_ref[c]
            m_new = jnp.maximum(m, jnp.max(s, axis=0, keepdims=True).astype(F32))
            e = jnp.exp2(s - m_new.astype(BF16))
            m_ref[c] = m_new
            acc_ref[c] = jnp.exp2(m - m_new) * acc_ref[c] + jnp.dot(vt, e, preferred_element_type=F32)

    m_ref[...] = jnp.full(m_ref.shape, -jnp.inf, F32)
    acc_ref[...] = jnp.zeros(acc_ref.shape, F32)
    s_bufs = (sa_ref, sb_ref)
    scores(0, s_bufs[0])

    def body(i, carry):
        j0 = ATTN_UNROLL * i
        for u in range(ATTN_UNROLL):
            scores(j0 + u + 1, s_bufs[(u + 1) % 2])
            accumulate(j0 + u, s_bufs[u % 2])
        return carry

    lax.fori_loop(0, n_chunks // ATTN_UNROLL - 1, body, 0)
    j0 = n_chunks - ATTN_UNROLL
    for u in range(ATTN_UNROLL):
        if u + 1 < ATTN_UNROLL:
            scores(j0 + u + 1, s_bufs[(u + 1) % 2])
        accumulate(j0 + u, s_bufs[u % 2])
    lam = lam_ref[0]
    a1, a2 = acc_ref[0], acc_ref[1]
    ot = a1[:HEAD_W, :] / a1[HEAD_W:HEAD_W + 1, :] - lam * (a2[:HEAD_W, :] / a2[HEAD_W:HEAD_W + 1, :])
    ms = jnp.mean(ot * ot, axis=0, keepdims=True)
    ot = ot * lax.rsqrt(ms + EPS) * (g_ref[...] * out_scale)
    o_ref[...] = ot.T.astype(BF16)


def _attn(lam, aq, ak, avt, g, batch, seq, out_scale):
    t = aq.shape[0]
    return pl.pallas_call(
        functools.partial(_attn_kernel, seq=seq, out_scale=out_scale),
        grid=(batch, N_HEADS),
        in_specs=[
            pl.BlockSpec(memory_space=pltpu.SMEM),
            pl.BlockSpec((seq, HEAD_W), lambda b, h: (b, h)),
            pl.BlockSpec((seq, HEAD_W), lambda b, h: (b, h)),
            pl.BlockSpec((VT_ROWS, seq), lambda b, h: (h, b)),
            pl.BlockSpec((HEAD_W, 1), lambda b, h: (0, 0)),
        ],
        out_specs=pl.BlockSpec((seq, HEAD_W), lambda b, h: (b, h)),
        out_shape=jax.ShapeDtypeStruct((t, ATTN_WIDTH), BF16),
        scratch_shapes=[
            pltpu.VMEM((2, seq, 2 * HEAD_W), F8),
            pltpu.VMEM((2, ATTN_TK, ATTN_TQ), BF16),
            pltpu.VMEM((2, ATTN_TK, ATTN_TQ), BF16),
            pltpu.VMEM((2, 1, ATTN_TQ), F32),
            pltpu.VMEM((2, VT_ROWS, ATTN_TQ), F32),
        ],
        compiler_params=pltpu.CompilerParams(
            dimension_semantics=("arbitrary", "arbitrary"), vmem_limit_bytes=VMEM_LIMIT),
        name="attn",
    )(lam, aq, ak, avt, g)


def _cumsum16(g, reverse):
    row = lax.broadcasted_iota(jnp.int32, g.shape, 0)
    b = g
    for sh in (1, 2, 4, 8):
        if reverse:
            b = b + jnp.where(row < HG_STEP - sh, pltpu.roll(b, HG_STEP - sh, axis=0), 0.0)
        else:
            b = b + jnp.where(row >= sh, pltpu.roll(b, sh, axis=0), 0.0)
    return b


def _hgrn_state(q, kk, b, v, st, reverse):
    qd = (q * jnp.exp2(b)).astype(BF16)
    o = lax.dot_general(qd, st.astype(BF16), (((1,), (1,)), ((), ())), preferred_element_type=F32)
    b_end = b[0:1, :] if reverse else b[HG_STEP - 1:HG_STEP, :]
    kd = (kk * jnp.exp2(b_end - b)).astype(BF16)
    upd = lax.dot_general(v, kd, (((0,), (0,)), ((), ())), preferred_element_type=F32)
    return o, st * jnp.exp2(b_end) + upd


def _hgrn_pairwise(q, kk, b, v, reverse):
    c = b - jnp.log2(kk)
    half = HG_STEP // 2
    row = lax.broadcasted_iota(jnp.int32, (half, HEAD_W), 0)
    vf = v.astype(F32)
    o = [jnp.zeros((half, HEAD_W), F32), jnp.zeros((half, HEAD_W), F32)]
    for s in range(HG_STEP):
        cs = c[s:s + 1, :]
        vs = vf[s:s + 1, :]
        for hi in range(2):
            r0 = hi * half
            if (r0 > s) if reverse else (r0 + half - 1 < s):
                continue
            d = b[r0:r0 + half, :] - cs
            if (r0 + half - 1 > s) if reverse else (r0 < s):
                seen = (row + r0 <= s) if reverse else (row + r0 >= s)
                d = jnp.where(seen, d, -jnp.inf)
            col = jnp.sum(q[r0:r0 + half, :] * jnp.exp2(d), axis=1, keepdims=True)
            o[hi] = o[hi] + col * vs
    return jnp.concatenate(o, axis=0)


def _hgrn_kernel(qf_ref, gf_ref, vf_ref, qb_ref, gb_ref, vb_ref, of_ref, ob_ref, sf_ref, sb_ref):
    @pl.when(pl.program_id(1) == 0)
    def _():
        sf_ref[...] = jnp.zeros_like(sf_ref)
        sb_ref[...] = jnp.zeros_like(sb_ref)

    nstep = HG_BLOCK // HG_STEP

    def body(i, carry):
        rf = pl.multiple_of(i * HG_STEP, HG_STEP)
        rb = pl.multiple_of((nstep - 1 - i) * HG_STEP, HG_STEP)
        chains = []
        for h in range(N_HEADS):
            cols = slice(h * HEAD_W, (h + 1) * HEAD_W)
            chains.append((qf_ref, gf_ref, vf_ref, of_ref, sf_ref, rf, h, cols, False))
            chains.append((qb_ref, gb_ref, vb_ref, ob_ref, sb_ref, rb, h, cols, True))
        live = []
        for q_ref, g_ref, v_ref, o_ref, s_ref, r, h, cols, reverse in chains:
            rows = pl.ds(r, HG_STEP)
            q, g, v = q_ref[rows, cols], g_ref[rows, cols], v_ref[rows, cols]
            kk = 1.0 - jnp.exp2(g)
            b = _cumsum16(g, reverse)
            o, st = _hgrn_state(q, kk, b, v, s_ref[h], reverse)
            o_ref[rows, cols] = o
            s_ref[h] = st
            live.append((q, kk, b, v))
        for (q_ref, g_ref, v_ref, o_ref, s_ref, r, h, cols, reverse), (q, kk, b, v) in zip(chains, live):
            rows = pl.ds(r, HG_STEP)
            o_ref[rows, cols] += _hgrn_pairwise(q, kk, b, v, reverse)
        return carry

    lax.fori_loop(0, nstep, body, 0)


def _hgrn(hq, gf, gb, hi, batch, seq):
    t = hq.shape[0]
    nb = seq // HG_BLOCK
    fwd = pl.BlockSpec((HG_BLOCK, HG_WIDTH), lambda b, j: (b * nb + j, 0))
    bwd = pl.BlockSpec((HG_BLOCK, HG_WIDTH), lambda b, j: (b * nb + nb - 1 - j, 0))
    return pl.pallas_call(
        _hgrn_kernel,
        grid=(batch, nb),
        in_specs=[fwd, fwd, fwd, bwd, bwd, bwd],
        out_specs=[fwd, bwd],
        out_shape=[jax.ShapeDtypeStruct((t, HG_WIDTH), F32)] * 2,
        scratch_shapes=[pltpu.VMEM((N_HEADS, HEAD_W, HEAD_W), F32)] * 2,
        compiler_params=pltpu.CompilerParams(
            dimension_semantics=("arbitrary", "arbitrary"), vmem_limit_bytes=VMEM_LIMIT),
        name="hgrn",
    )(hq, gf, hi, hq, gb, hi)


def _out_proj_ffn_kernel(x_ref, ao_ref, of_ref, ob_ref, gate_ref, hgg_ref, w_ref, g1_ref, b1_ref,
                         wg_ref, wu_ref, wd_ref, g2_ref, b2_ref, o_ref):
    x = x_ref[...]
    ho = of_ref[...] + ob_ref[...]
    gate = gate_ref[...]
    hgg = hgg_ref[...]
    mix = jnp.dot(ao_ref[...], w_ref[0:ATTN_WIDTH, :], preferred_element_type=F32)
    for h in range(N_HEADS):
        cols = slice(h * HEAD_W, (h + 1) * HEAD_W)
        oh = ho[:, cols]
        ms = jnp.mean(oh * oh, axis=-1, keepdims=True)
        nh = (oh * lax.rsqrt(ms + EPS) * hgg * gate[:, cols]).astype(BF16)
        mix = mix + jnp.dot(nh, w_ref[ATTN_WIDTH + h * HEAD_W:ATTN_WIDTH + (h + 1) * HEAD_W, :],
                            preferred_element_type=F32)
    x = _layer_norm(ALPHA * x + mix, g1_ref[...], b1_ref[...])
    o_ref[...] = _ffn_block(x, wg_ref, wu_ref, wd_ref, g2_ref[...], b2_ref[...])


def _out_proj_ffn(x, ao, of, ob, gate, hgg, w_out, g1, b1, wg, wu, wd, g2, b2):
    t = x.shape[0]
    row = lambda i: (i, 0)
    const = lambda i: (0, 0)
    grp = pl.BlockSpec((ROW_TILE, 512), row)
    vec = pl.BlockSpec((1, D_MODEL), const)
    return pl.pallas_call(
        _out_proj_ffn_kernel,
        grid=(t // ROW_TILE,),
        in_specs=[
            pl.BlockSpec((ROW_TILE, D_MODEL), row),
            grp, grp, grp, grp,
            pl.BlockSpec((1, HEAD_W), const),
            pl.BlockSpec((D_MODEL, D_MODEL), const, pipeline_mode=pl.Buffered(1)),
            vec, vec,
            pl.BlockSpec((D_MODEL, D_FF), const, pipeline_mode=pl.Buffered(1)),
            pl.BlockSpec((D_MODEL, D_FF), const, pipeline_mode=pl.Buffered(1)),
            pl.BlockSpec((D_FF, D_MODEL), const, pipeline_mode=pl.Buffered(1)),
            vec, vec,
        ],
        out_specs=pl.BlockSpec((ROW_TILE, D_MODEL), row),
        out_shape=jax.ShapeDtypeStruct((t, D_MODEL), F32),
        compiler_params=pltpu.CompilerParams(
            dimension_semantics=("arbitrary",), vmem_limit_bytes=VMEM_LIMIT),
        name="out_proj_ffn",
    )(x, ao, of, ob, gate, hgg, w_out, g1, b1, wg, wu, wd, g2, b2)


def _rope_tables(seq):
    half = ROT_DIM // 2
    inv_freq = jnp.float32(ROPE_THETA) ** (-jnp.arange(half, dtype=F32) * 2.0 / ROT_DIM)
    ang = jnp.arange(seq, dtype=F32)[:, None] * inv_freq[None, :]
    cos, sin = jnp.cos(ang), jnp.sin(ang)
    ones = jnp.ones((seq, ATTN_HEAD_DIM - ROT_DIM), F32)
    zeros = jnp.zeros((seq, ATTN_HEAD_DIM - ROT_DIM), F32)
    z8 = jnp.zeros((seq, half), F32)
    comp_c = jnp.concatenate([cos, cos, ones], axis=1)
    comp_a = jnp.concatenate([-sin, z8, zeros], axis=1)
    comp_b = jnp.concatenate([z8, sin, zeros], axis=1)
    two = lambda m: jnp.concatenate([m, m], axis=1)
    return two(comp_c), two(comp_a), two(comp_b)


def _trunk(x3, p):
    batch, seq, _ = x3.shape
    x = x3.reshape(batch * seq, D_MODEL)
    rope_c, rope_a, rope_b = _rope_tables(seq)
    for l in range(DEPTH):
        lam_init = 0.8 - 0.6 * math.exp(-0.3 * l)
        x = _ffn(x, p["wg"][l][0], p["wu"][l][0], p["wd"][l][0], p["ln_g"][l][0], p["ln_b"][l][0])
        aq, ak, avt, hq, gf, gb, hi, gate = _in_proj(
            x, p["w_in"][l], p["wvt"][l], rope_c, rope_a, rope_b, p["lbs"][0][l], p["lbs"][1][l], seq)
        ao = _attn(p["lam"][l], aq, ak, avt, p["attn_g"][l], batch, seq, 1.0 - lam_init)
        of, ob = _hgrn(hq, gf, gb, hi, batch, seq)
        x = _out_proj_ffn(x, ao, of, ob, gate, p["hg_g"][l], p["w_out"][l], p["ln_g"][l][1], p["ln_b"][l][1],
                          p["wg"][l][1], p["wu"][l][1], p["wd"][l][1], p["ln_g"][l][2], p["ln_b"][l][2])
    return x.reshape(batch, seq, D_MODEL)


def kernel(x_prompt, x_sample, w_in, w_out, attn_lambda, attn_norm_g, hg_norm_g, hg_lower_bound,
           ffn_w_gate, ffn_w_up, ffn_w_down, ln_g, ln_b):
    sm = jax.nn.softmax(hg_lower_bound.astype(F32), axis=1)
    lbs = jnp.maximum(jnp.cumsum(sm, axis=1) - sm[:, :1], 0.0)
    lp = attn_lambda.astype(F32)
    lam_init = jnp.asarray([0.8 - 0.6 * math.exp(-0.3 * l) for l in range(DEPTH)], F32)
    lam = (jnp.exp(jnp.sum(lp[:, 0] * lp[:, 1], axis=-1))
           - jnp.exp(jnp.sum(lp[:, 2] * lp[:, 3], axis=-1)) + lam_init)
    p = {
        "w_in": w_in.astype(BF16), "w_out": w_out.astype(BF16),
        "wvt": jnp.swapaxes(w_in[:, :, 2 * ATTN_WIDTH:3 * ATTN_WIDTH], 1, 2).astype(BF16),
        "wg": ffn_w_gate.astype(BF16), "wu": ffn_w_up.astype(BF16), "wd": ffn_w_down.astype(BF16),
        "ln_g": ln_g.reshape(DEPTH, 3, 1, D_MODEL), "ln_b": ln_b.reshape(DEPTH, 3, 1, D_MODEL),
        "attn_g": attn_norm_g.reshape(DEPTH, HEAD_W, 1), "hg_g": hg_norm_g.reshape(DEPTH, 1, HEAD_W),
        "lbs": lbs.reshape(2, DEPTH, 1, HG_WIDTH), "lam": lam.reshape(DEPTH, 1),
    }
    return (_trunk(x_prompt, p), _trunk(x_sample, p))
```

```python
import functools
import math

import jax
import jax.numpy as jnp
from jax import lax
from jax.experimental import pallas as pl
from jax.experimental.pallas import tpu as pltpu

D_MODEL = 1024
DEPTH = 2
ATTN_WIDTH = 512
HG_WIDTH = 512
HEAD_W = 128
N_HEADS = 4
ATTN_HEAD_DIM = 64
ROT_DIM = 16
ROPE_THETA = 500000.0
D_FF = 2816
ALPHA = (2 * DEPTH) ** 0.25
EPS = 1e-5
IN_WIDTH = 3 * ATTN_WIDTH + 5 * HG_WIDTH
LOG2E = math.log2(math.e)
ATTN_Q_SCALE = ATTN_HEAD_DIM ** -0.5 * LOG2E

ROW_TILE = 512
FFN_ROW_TILE = 2 * ROW_TILE
FF_CHUNK = 512
ATTN_TQ = 512
ATTN_UNROLL = 2
ATTN_TK = 512
VT_ROWS = HEAD_W + 16
HG_BLOCK = 512
HG_STEP = 16
VMEM_LIMIT = 56 * 1024 * 1024

BF16 = jnp.bfloat16
F32 = jnp.float32
F8 = jnp.float8_e4m3fn
F8_BLOCK_MAX = 128.0
F32_EXPONENT_MASK = 0x7F800000
F8_MIN_BLOCK_MAX = 1e-30


def _layer_norm(y, g, b):
    mu = jnp.mean(y, axis=-1, keepdims=True)
    yc = y - mu
    var = jnp.mean(yc * yc, axis=-1, keepdims=True)
    return yc * lax.rsqrt(var + EPS) * g + b


def _sigmoid(z):
    return 1.0 / (1.0 + jnp.exp(-z))


def _ffn_block(x, wg_ref, wu_ref, wd_ref, g, b):
    xb = x.astype(BF16)
    acc = jnp.zeros((x.shape[0], D_MODEL), F32)
    for f0 in range(0, D_FF, FF_CHUNK):
        f1 = min(f0 + FF_CHUNK, D_FF)
        hg = jnp.dot(xb, wg_ref[:, f0:f1], preferred_element_type=F32)
        hu = jnp.dot(xb, wu_ref[:, f0:f1], preferred_element_type=F32)
        h = (hg * _sigmoid(hg) * hu).astype(BF16)
        acc = acc + jnp.dot(h, wd_ref[f0:f1, :], preferred_element_type=F32)
    return _layer_norm(ALPHA * x + 0.5 * acc, g, b)


def _ffn_kernel(x_ref, wg_ref, wu_ref, wd_ref, g_ref, b_ref, o_ref):
    for r in range(0, FFN_ROW_TILE, ROW_TILE):
        o_ref[r:r + ROW_TILE, :] = _ffn_block(x_ref[r:r + ROW_TILE, :], wg_ref, wu_ref, wd_ref,
                                              g_ref[...], b_ref[...])


def _stacked(shape, *lead):
    return pl.BlockSpec((None,) * len(lead) + shape, lambda i: lead + (0,) * len(shape),
                        pipeline_mode=pl.Buffered(1))


def _ffn(x, wg, wu, wd, g, b, l, j):
    t = x.shape[0]
    row = lambda i: (i, 0)
    const = lambda i: (0, 0)
    return pl.pallas_call(
        _ffn_kernel,
        grid=(t // FFN_ROW_TILE,),
        in_specs=[
            pl.BlockSpec((FFN_ROW_TILE, D_MODEL), row),
            _stacked((D_MODEL, D_FF), l, j),
            _stacked((D_MODEL, D_FF), l, j),
            _stacked((D_FF, D_MODEL), l, j),
            pl.BlockSpec((1, D_MODEL), const),
            pl.BlockSpec((1, D_MODEL), const),
        ],
        out_specs=pl.BlockSpec((FFN_ROW_TILE, D_MODEL), row),
        out_shape=jax.ShapeDtypeStruct((t, D_MODEL), F32),
        compiler_params=pltpu.CompilerParams(
            dimension_semantics=("arbitrary",), vmem_limit_bytes=VMEM_LIMIT),
        name="ffn",
    )(x, wg, wu, wd, g, b)


def _rope(x, c, a, bt):
    outs = []
    for h in range(N_HEADS):
        xh = x[:, h * HEAD_W:(h + 1) * HEAD_W]
        up = pltpu.roll(xh, HEAD_W - ROT_DIM // 2, axis=1)
        dn = pltpu.roll(xh, ROT_DIM // 2, axis=1)
        outs.append(xh * c + up * a + dn * bt)
    return outs


def _log_forget(z, lb):
    z2 = z * LOG2E
    log_sig = jnp.minimum(z2, 0.0) - jnp.log2(1.0 + jnp.exp2(-jnp.abs(z2)))
    a = jnp.log2(lb)
    c = jnp.log2(1.0 - lb) + log_sig
    log_f = jnp.maximum(a, c) + jnp.log2(1.0 + jnp.exp2(-jnp.abs(a - c)))
    return jnp.minimum(log_f, 0.0)


def _in_proj_kernel(x_ref, w_ref, wvt_ref, c_ref, a_ref, bt_ref, lbf_ref, lbb_ref,
                    aq_ref, ak_ref, avt_ref, hq_ref, gf_ref, gb_ref, hi_ref, gate_ref):
    xb = x_ref[...].astype(BF16)
    c, a, bt = c_ref[...], a_ref[...], bt_ref[...]

    def proj(j):
        return jnp.dot(xb, w_ref[:, j * 512:(j + 1) * 512], preferred_element_type=F32)

    gf_ref[...] = _log_forget(proj(4), lbf_ref[...])
    gb_ref[...] = _log_forget(proj(5), lbb_ref[...])
    q = _rope(proj(0), c, a, bt)
    for h in range(N_HEADS):
        aq_ref[:, h * HEAD_W:(h + 1) * HEAD_W] = (q[h] * ATTN_Q_SCALE).astype(BF16)
    k = _rope(proj(1), c, a, bt)
    for h in range(N_HEADS):
        ak_ref[:, h * HEAD_W:(h + 1) * HEAD_W] = k[h].astype(BF16)
    hq = proj(3)
    hq_ref[...] = hq * _sigmoid(hq)
    gt = proj(7)
    gate_ref[...] = gt * _sigmoid(gt)
    vt = lax.dot_general(wvt_ref[...], xb, (((1,), (1,)), ((), ())), preferred_element_type=F32).astype(BF16)
    ones = jnp.ones((VT_ROWS - HEAD_W, vt.shape[1]), BF16)
    for h in range(N_HEADS):
        avt_ref[h * VT_ROWS:h * VT_ROWS + HEAD_W, :] = vt[h * HEAD_W:(h + 1) * HEAD_W, :]
        avt_ref[h * VT_ROWS + HEAD_W:(h + 1) * VT_ROWS, :] = ones
    hi_ref[...] = proj(6).astype(BF16)


def _in_proj(x, w_in, wvt, rope_c, rope_a, rope_b, lb_f, lb_b, seq, l):
    t = x.shape[0]
    nseq = seq // ROW_TILE
    row = lambda i: (i, 0)
    const = lambda i: (0, 0)
    pos = lambda i: (i % nseq, 0)
    grp = pl.BlockSpec((ROW_TILE, 512), row)
    shp = lambda dt: jax.ShapeDtypeStruct((t, 512), dt)
    return pl.pallas_call(
        _in_proj_kernel,
        grid=(t // ROW_TILE,),
        in_specs=[
            pl.BlockSpec((ROW_TILE, D_MODEL), row),
            _stacked((D_MODEL, IN_WIDTH), l),
            _stacked((ATTN_WIDTH, D_MODEL), l),
            pl.BlockSpec((ROW_TILE, HEAD_W), pos),
            pl.BlockSpec((ROW_TILE, HEAD_W), pos),
            pl.BlockSpec((ROW_TILE, HEAD_W), pos),
            pl.BlockSpec((1, HG_WIDTH), const),
            pl.BlockSpec((1, HG_WIDTH), const),
        ],
        out_specs=[grp, grp, pl.BlockSpec((N_HEADS * VT_ROWS, ROW_TILE), lambda i: (0, i)),
                   grp, grp, grp, grp, grp],
        out_shape=[shp(BF16), shp(BF16), jax.ShapeDtypeStruct((N_HEADS * VT_ROWS, t), BF16),
                   shp(F32), shp(F32), shp(F32), shp(BF16), shp(F32)],
        compiler_params=pltpu.CompilerParams(
            dimension_semantics=("arbitrary",), vmem_limit_bytes=VMEM_LIMIT),
        name="in_proj",
    )(x, w_in, wvt, rope_c, rope_a, rope_b, lb_f, lb_b)


def _component_max(a):
    first = lax.broadcasted_iota(jnp.int32, a.shape, 1) < ATTN_HEAD_DIM
    m0 = jnp.max(jnp.where(first, a, 0.0), axis=1, keepdims=True)
    m1 = jnp.max(jnp.where(first, 0.0, a), axis=1, keepdims=True)
    return jnp.maximum(jnp.where(first, m0, m1), F8_MIN_BLOCK_MAX)


def _f8_scale(block_max):
    bits = lax.bitcast_convert_type(F8_BLOCK_MAX / block_max, jnp.int32)
    return lax.bitcast_convert_type(bits & F32_EXPONENT_MASK, F32)


def _f8_features(x, keys):
    hi = x.astype(F8).astype(F32)
    lo = x - hi
    first = lax.broadcasted_iota(jnp.int32, x.shape, 1) < ATTN_HEAD_DIM
    hi_sw = pltpu.roll(hi, ATTN_HEAD_DIM, axis=1)
    lo_sw = pltpu.roll(lo, ATTN_HEAD_DIM, axis=1)
    zero = jnp.zeros_like(x)
    if keys:
        groups = ((jnp.where(first, hi, lo_sw), jnp.where(first, hi, zero)),
                  (jnp.where(first, hi_sw, lo), jnp.where(first, hi_sw, zero)))
    else:
        groups = ((jnp.where(first, hi, hi_sw), jnp.where(first, lo, zero)),
                  (jnp.where(first, hi_sw, hi), jnp.where(first, lo_sw, zero)))
    return [jnp.concatenate(g, axis=1).astype(F8) for g in groups]


def _attn_kernel(lam_ref, q_ref, k_ref, vt_ref, g_ref, o_ref, kf_ref, sa_ref, sb_ref, m_ref, acc_ref, *,
                 seq, out_scale):
    n_chunks = seq // ATTN_TK

    def chunk(j):
        return pl.ds(pl.multiple_of(j * ATTN_TK, ATTN_TK), ATTN_TK)

    def kmax_body(j, a):
        return jnp.maximum(a, jnp.max(jnp.abs(k_ref[chunk(j), :].astype(F32)), axis=0, keepdims=True))

    kscale = _f8_scale(_component_max(lax.fori_loop(0, n_chunks, kmax_body, jnp.zeros((1, HEAD_W), F32))))

    def kfeat_body(j, carry):
        feats = _f8_features(k_ref[chunk(j), :].astype(F32) * kscale, keys=True)
        for c in range(2):
            kf_ref[c, chunk(j), :] = feats[c]
        return carry

    lax.fori_loop(0, n_chunks, kfeat_body, 0)

    def qblock(i, carry):
        rows = pl.ds(pl.multiple_of(i * ATTN_TQ, ATTN_TQ), ATTN_TQ)
        _attn_qblock(lam_ref, q_ref[rows, :], kscale, kf_ref, vt_ref, g_ref, o_ref.at[rows, :],
                     sa_ref, sb_ref, m_ref, acc_ref, seq=seq, out_scale=out_scale)
        return carry

    lax.fori_loop(0, seq // ATTN_TQ, qblock, 0)


def _attn_qblock(lam_ref, q, kscale, kf_ref, vt_ref, g_ref, o_ref, sa_ref, sb_ref, m_ref, acc_ref, *, seq, out_scale):
    qf32 = q.astype(F32)
    qscale = _f8_scale(_component_max(jnp.max(jnp.abs(qf32), axis=0, keepdims=True)))
    qf = _f8_features(qf32 * qscale, keys=False)
    unscale = ((1.0 / kscale) * (1.0 / qscale)).astype(BF16)
    unscale = (unscale[:, 0:1], unscale[:, HEAD_W - 1:HEAD_W])
    nt = (((1,), (1,)), ((), ()))
    n_chunks = seq // ATTN_TK

    def chunk(j):
        return pl.ds(pl.multiple_of(j * ATTN_TK, ATTN_TK), ATTN_TK)

    def scores(j, s_ref):
        for c in range(2):
            s = lax.dot_general(kf_ref[c, chunk(j), :], qf[c], nt, preferred_element_type=F32)
            s_ref[c] = s.astype(BF16) * unscale[c]

    def accumulate(j, s_ref):
        vt = vt_ref[:, chunk(j)]
        for c in range(2):
            s = s_ref[c]
            m = m_ref[c]
            m_new = jnp.maximum(m, jnp.max(s, axis=0, keepdims=True).astype(F32))
            e = jnp.exp2(s - m_new.astype(BF16))
            m_ref[c] = m_new
            acc_ref[c] = jnp.exp2(m - m_new) * acc_ref[c] + jnp.dot(vt, e, preferred_element_type=F32)

    m_ref[...] = jnp.full(m_ref.shape, -jnp.inf, F32)
    acc_ref[...] = jnp.zeros(acc_ref.shape, F32)
    s_bufs = (sa_ref, sb_ref)
    scores(0, s_bufs[0])

    def body(i, carry):
        j0 = ATTN_UNROLL * i
        for u in range(ATTN_UNROLL):
            scores(j0 + u + 1, s_bufs[(u + 1) % 2])
            accumulate(j0 + u, s_bufs[u % 2])
        return carry

    lax.fori_loop(0, n_chunks // ATTN_UNROLL - 1, body, 0)
    j0 = n_chunks - ATTN_UNROLL
    for u in range(ATTN_UNROLL):
        if u + 1 < ATTN_UNROLL:
            scores(j0 + u + 1, s_bufs[(u + 1) % 2])
        accumulate(j0 + u, s_bufs[u % 2])
    lam = lam_ref[0]
    a1, a2 = acc_ref[0], acc_ref[1]
    ot = a1[:HEAD_W, :] / a1[HEAD_W:HEAD_W + 1, :] - lam * (a2[:HEAD_W, :] / a2[HEAD_W:HEAD_W + 1, :])
    ms = jnp.mean(ot * ot, axis=0, keepdims=True)
    ot = ot * lax.rsqrt(ms + EPS) * (g_ref[...] * out_scale)
    o_ref[...] = ot.T.astype(BF16)


def _attn(lam, aq, ak, avt, g, batch, seq, out_scale):
    t = aq.shape[0]
    return pl.pallas_call(
        functools.partial(_attn_kernel, seq=seq, out_scale=out_scale),
        grid=(batch, N_HEADS),
        in_specs=[
            pl.BlockSpec(memory_space=pltpu.SMEM),
            pl.BlockSpec((seq, HEAD_W), lambda b, h: (b, h)),
            pl.BlockSpec((seq, HEAD_W), lambda b, h: (b, h)),
            pl.BlockSpec((VT_ROWS, seq), lambda b, h: (h, b)),
            pl.BlockSpec((HEAD_W, 1), lambda b, h: (0, 0)),
        ],
        out_specs=pl.BlockSpec((seq, HEAD_W), lambda b, h: (b, h)),
        out_shape=jax.ShapeDtypeStruct((t, ATTN_WIDTH), BF16),
        scratch_shapes=[
            pltpu.VMEM((2, seq, 2 * HEAD_W), F8),
            pltpu.VMEM((2, ATTN_TK, ATTN_TQ), BF16),
            pltpu.VMEM((2, ATTN_TK, ATTN_TQ), BF16),
            pltpu.VMEM((2, 1, ATTN_TQ), F32),
            pltpu.VMEM((2, VT_ROWS, ATTN_TQ), F32),
        ],
        compiler_params=pltpu.CompilerParams(
            dimension_semantics=("arbitrary", "arbitrary"), vmem_limit_bytes=VMEM_LIMIT),
        name="attn",
    )(lam, aq, ak, avt, g)


def _cumsum16(g, reverse):
    row = lax.broadcasted_iota(jnp.int32, g.shape, 0)
    b = g
    for sh in (1, 2, 4, 8):
        if reverse:
            b = b + jnp.where(row < HG_STEP - sh, pltpu.roll(b, HG_STEP - sh, axis=0), 0.0)
        else:
            b = b + jnp.where(row >= sh, pltpu.roll(b, sh, axis=0), 0.0)
    return b


def _hgrn_state(q, kk, b, v, st, reverse):
    qd = (q * jnp.exp2(b)).astype(BF16)
    o = lax.dot_general(qd, st.astype(BF16), (((1,), (1,)), ((), ())), preferred_element_type=F32)
    b_end = b[0:1, :] if reverse else b[HG_STEP - 1:HG_STEP, :]
    kd = (kk * jnp.exp2(b_end - b)).astype(BF16)
    upd = lax.dot_general(v, kd, (((0,), (0,)), ((), ())), preferred_element_type=F32)
    return o, st * jnp.exp2(b_end) + upd


def _hgrn_pairwise(q, kk, b, v, reverse):
    c = b - jnp.log2(kk)
    half = HG_STEP // 2
    row = lax.broadcasted_iota(jnp.int32, (half, HEAD_W), 0)
    vf = v.astype(F32)
    o = [jnp.zeros((half, HEAD_W), F32), jnp.zeros((half, HEAD_W), F32)]
    for s in range(HG_STEP):
        cs = c[s:s + 1, :]
        vs = vf[s:s + 1, :]
        for hi in range(2):
            r0 = hi * half
            if (r0 > s) if reverse else (r0 + half - 1 < s):
                continue
            d = b[r0:r0 + half, :] - cs
            if (r0 + half - 1 > s) if reverse else (r0 < s):
                seen = (row + r0 <= s) if reverse else (row + r0 >= s)
                d = jnp.where(seen, d, -jnp.inf)
            col = jnp.sum(q[r0:r0 + half, :] * jnp.exp2(d), axis=1, keepdims=True)
            o[hi] = o[hi] + col * vs
    return jnp.concatenate(o, axis=0)


def _hgrn_kernel(qf_ref, gf_ref, vf_ref, qb_ref, gb_ref, vb_ref, of_ref, ob_ref, sf_ref, sb_ref):
    @pl.when(pl.program_id(1) == 0)
    def _():
        sf_ref[...] = jnp.zeros_like(sf_ref)
        sb_ref[...] = jnp.zeros_like(sb_ref)

    nstep = HG_BLOCK // HG_STEP

    def body(i, carry):
        rf = pl.multiple_of(i * HG_STEP, HG_STEP)
        rb = pl.multiple_of((nstep - 1 - i) * HG_STEP, HG_STEP)
        chains = []
        for h in range(N_HEADS):
            cols = slice(h * HEAD_W, (h + 1) * HEAD_W)
            chains.append((qf_ref, gf_ref, vf_ref, of_ref, sf_ref, rf, h, cols, False))
            chains.append((qb_ref, gb_ref, vb_ref, ob_ref, sb_ref, rb, h, cols, True))
        live = []
        for q_ref, g_ref, v_ref, o_ref, s_ref, r, h, cols, reverse in chains:
            rows = pl.ds(r, HG_STEP)
            q, g, v = q_ref[rows, cols], g_ref[rows, cols], v_ref[rows, cols]
            kk = 1.0 - jnp.exp2(g)
            b = _cumsum16(g, reverse)
            o, st = _hgrn_state(q, kk, b, v, s_ref[h], reverse)
            o_ref[rows, cols] = o
            s_ref[h] = st
            live.append((q, kk, b, v))
        for (q_ref, g_ref, v_ref, o_ref, s_ref, r, h, cols, reverse), (q, kk, b, v) in zip(chains, live):
            rows = pl.ds(r, HG_STEP)
            o_ref[rows, cols] += _hgrn_pairwise(q, kk, b, v, reverse)
        return carry

    lax.fori_loop(0, nstep, body, 0)


def _hgrn(hq, gf, gb, hi, batch, seq):
    t = hq.shape[0]
    nb = seq // HG_BLOCK
    fwd = pl.BlockSpec((HG_BLOCK, HG_WIDTH), lambda b, j: (b * nb + j, 0))
    bwd = pl.BlockSpec((HG_BLOCK, HG_WIDTH), lambda b, j: (b * nb + nb - 1 - j, 0))
    return pl.pallas_call(
        _hgrn_kernel,
        grid=(batch, nb),
        in_specs=[fwd, fwd, fwd, bwd, bwd, bwd],
        out_specs=[fwd, bwd],
        out_shape=[jax.ShapeDtypeStruct((t, HG_WIDTH), F32)] * 2,
        scratch_shapes=[pltpu.VMEM((N_HEADS, HEAD_W, HEAD_W), F32)] * 2,
        compiler_params=pltpu.CompilerParams(
            dimension_semantics=("arbitrary", "arbitrary"), vmem_limit_bytes=VMEM_LIMIT),
        name="hgrn",
    )(hq, gf, hi, hq, gb, hi)


def _out_proj_ffn_kernel(x_ref, ao_ref, of_ref, ob_ref, gate_ref, hgg_ref, w_ref, g1_ref, b1_ref,
                         wg_ref, wu_ref, wd_ref, g2_ref, b2_ref, o_ref):
    x = x_ref[...]
    ho = of_ref[...] + ob_ref[...]
    gate = gate_ref[...]
    hgg = hgg_ref[...]
    mix = jnp.dot(ao_ref[...], w_ref[0:ATTN_WIDTH, :], preferred_element_type=F32)
    for h in range(N_HEADS):
        cols = slice(h * HEAD_W, (h + 1) * HEAD_W)
        oh = ho[:, cols]
        ms = jnp.mean(oh * oh, axis=-1, keepdims=True)
        nh = (oh * lax.rsqrt(ms + EPS) * hgg * gate[:, cols]).astype(BF16)
        mix = mix + jnp.dot(nh, w_ref[ATTN_WIDTH + h * HEAD_W:ATTN_WIDTH + (h + 1) * HEAD_W, :],
                            preferred_element_type=F32)
    x = _layer_norm(ALPHA * x + mix, g1_ref[...], b1_ref[...])
    o_ref[...] = _ffn_block(x, wg_ref, wu_ref, wd_ref, g2_ref[...], b2_ref[...])


def _out_proj_ffn(x, ao, of, ob, gate, hgg, w_out, g1, b1, wg, wu, wd, g2, b2, l):
    t = x.shape[0]
    row = lambda i: (i, 0)
    const = lambda i: (0, 0)
    grp = pl.BlockSpec((ROW_TILE, 512), row)
    vec = pl.BlockSpec((1, D_MODEL), const)
    return pl.pallas_call(
        _out_proj_ffn_kernel,
        grid=(t // ROW_TILE,),
        in_specs=[
            pl.BlockSpec((ROW_TILE, D_MODEL), row),
            grp, grp, grp, grp,
            pl.BlockSpec((1, HEAD_W), const),
            _stacked((D_MODEL, D_MODEL), l),
            vec, vec,
            _stacked((D_MODEL, D_FF), l, 1),
            _stacked((D_MODEL, D_FF), l, 1),
            _stacked((D_FF, D_MODEL), l, 1),
            vec, vec,
        ],
        out_specs=pl.BlockSpec((ROW_TILE, D_MODEL), row),
        out_shape=jax.ShapeDtypeStruct((t, D_MODEL), F32),
        compiler_params=pltpu.CompilerParams(
            dimension_semantics=("arbitrary",), vmem_limit_bytes=VMEM_LIMIT),
        name="out_proj_ffn",
    )(x, ao, of, ob, gate, hgg, w_out, g1, b1, wg, wu, wd, g2, b2)


def _rope_tables(seq):
    half = ROT_DIM // 2
    lane = jnp.arange(HEAD_W) % ATTN_HEAD_DIM
    inv_freq = jnp.float32(ROPE_THETA) ** (-(lane % half).astype(F32) * 2.0 / ROT_DIM)
    ang = jnp.arange(seq, dtype=F32)[:, None] * inv_freq[None, :]
    cos, sin = jnp.cos(ang), jnp.sin(ang)
    rope_c = jnp.where(lane < ROT_DIM, cos, 1.0)
    rope_a = jnp.where(lane < half, -sin, 0.0)
    rope_b = jnp.where((lane >= half) & (lane < ROT_DIM), sin, 0.0)
    return rope_c, rope_a, rope_b


def _trunk(x3, p):
    batch, seq, _ = x3.shape
    x = x3.reshape(batch * seq, D_MODEL)
    rope_c, rope_a, rope_b = _rope_tables(seq)
    for l in range(DEPTH):
        lam_init = 0.8 - 0.6 * math.exp(-0.3 * l)
        x = _ffn(x, p["wg"], p["wu"], p["wd"], p["ln_g"][l][0], p["ln_b"][l][0], l, 0)
        aq, ak, avt, hq, gf, gb, hi, gate = _in_proj(
            x, p["w_in"], p["wvt"], rope_c, rope_a, rope_b, p["lbs"][0][l], p["lbs"][1][l], seq, l)
        ao = _attn(p["lam"][l], aq, ak, avt, p["attn_g"][l], batch, seq, 1.0 - lam_init)
        of, ob = _hgrn(hq, gf, gb, hi, batch, seq)
        x = _out_proj_ffn(x, ao, of, ob, gate, p["hg_g"][l], p["w_out"], p["ln_g"][l][1], p["ln_b"][l][1],
                          p["wg"], p["wu"], p["wd"], p["ln_g"][l][2], p["ln_b"][l][2], l)
    return x.reshape(batch, seq, D_MODEL)


def kernel(x_prompt, x_sample, w_in, w_out, attn_lambda, attn_norm_g, hg_norm_g, hg_lower_bound,
           ffn_w_gate, ffn_w_up, ffn_w_down, ln_g, ln_b):
    sm = jax.nn.softmax(hg_lower_bound.astype(F32), axis=1)
    lbs = jnp.maximum(jnp.cumsum(sm, axis=1) - sm[:, :1], 0.0)
    lp = attn_lambda.astype(F32)
    lam_init = jnp.asarray([0.8 - 0.6 * math.exp(-0.3 * l) for l in range(DEPTH)], F32)
    lam = (jnp.exp(jnp.sum(lp[:, 0] * lp[:, 1], axis=-1))
           - jnp.exp(jnp.sum(lp[:, 2] * lp[:, 3], axis=-1)) + lam_init)
    p = {
        "w_in": w_in.astype(BF16), "w_out": w_out.astype(BF16),
        "wvt": jnp.swapaxes(w_in[:, :, 2 * ATTN_WIDTH:3 * ATTN_WIDTH], 1, 2).astype(BF16),
        "wg": ffn_w_gate.astype(BF16), "wu": ffn_w_up.astype(BF16), "wd": ffn_w_down.astype(BF16),
        "ln_g": ln_g.reshape(DEPTH, 3, 1, D_MODEL), "ln_b": ln_b.reshape(DEPTH, 3, 1, D_MODEL),
        "attn_g": attn_norm_g.reshape(DEPTH, HEAD_W, 1), "hg_g": hg_norm_g.reshape(DEPTH, 1, HEAD_W),
        "lbs": lbs.reshape(2, DEPTH, 1, HG_WIDTH), "lam": lam.reshape(DEPTH, 1),
    }
    return (_trunk(x_prompt, p), _trunk(x_sample, p))
```

```python
import functools
import math

import jax
import jax.numpy as jnp
from jax import lax
from jax.experimental import pallas as pl
from jax.experimental.pallas import tpu as pltpu

D_MODEL = 1024
DEPTH = 2
ATTN_WIDTH = 512
HG_WIDTH = 512
HEAD_W = 128
N_HEADS = 4
ATTN_HEAD_DIM = 64
ROT_DIM = 16
ROPE_THETA = 500000.0
D_FF = 2816
ALPHA = (2 * DEPTH) ** 0.25
EPS = 1e-5
IN_WIDTH = 3 * ATTN_WIDTH + 5 * HG_WIDTH
LOG2E = math.log2(math.e)
ATTN_Q_SCALE = ATTN_HEAD_DIM ** -0.5 * LOG2E

ROW_TILE = 512
FFN_ROW_TILE = 2 * ROW_TILE
FF_CHUNK = 512
ATTN_TQ = 512
ATTN_UNROLL = 2
ATTN_TK = 512
VT_ROWS = HEAD_W + 16
HG_BLOCK = 512
HG_STEP = 16
VMEM_LIMIT = 56 * 1024 * 1024

BF16 = jnp.bfloat16
F32 = jnp.float32
F8 = jnp.float8_e4m3fn
F8_BLOCK_MAX = 128.0
F32_EXPONENT_MASK = 0x7F800000
F8_MIN_BLOCK_MAX = 1e-30


def _layer_norm(y, g, b):
    mu = jnp.mean(y, axis=-1, keepdims=True)
    yc = y - mu
    var = jnp.mean(yc * yc, axis=-1, keepdims=True)
    return yc * lax.rsqrt(var + EPS) * g + b


def _sigmoid(z):
    return 1.0 / (1.0 + jnp.exp(-z))


def _ffn_block(x, wg_ref, wu_ref, wd_ref, g, b):
    xb = x.astype(BF16)
    acc = jnp.zeros((x.shape[0], D_MODEL), F32)
    for f0 in range(0, D_FF, FF_CHUNK):
        f1 = min(f0 + FF_CHUNK, D_FF)
        hg = jnp.dot(xb, wg_ref[:, f0:f1], preferred_element_type=F32)
        hu = jnp.dot(xb, wu_ref[:, f0:f1], preferred_element_type=F32)
        h = (hg * _sigmoid(hg) * hu).astype(BF16)
        acc = acc + jnp.dot(h, wd_ref[f0:f1, :], preferred_element_type=F32)
    return _layer_norm(ALPHA * x + 0.5 * acc, g, b)


def _ffn_kernel(x_ref, wg_ref, wu_ref, wd_ref, g_ref, b_ref, o_ref):
    for r in range(0, FFN_ROW_TILE, ROW_TILE):
        o_ref[r:r + ROW_TILE, :] = _ffn_block(x_ref[r:r + ROW_TILE, :], wg_ref, wu_ref, wd_ref,
                                              g_ref[...], b_ref[...])


def _stacked(shape, *lead):
    return pl.BlockSpec((None,) * len(lead) + shape, lambda i: lead + (0,) * len(shape),
                        pipeline_mode=pl.Buffered(1))


def _ffn(x, wg, wu, wd, g, b, l, j):
    t = x.shape[0]
    row = lambda i: (i, 0)
    const = lambda i: (0, 0)
    return pl.pallas_call(
        _ffn_kernel,
        grid=(t // FFN_ROW_TILE,),
        in_specs=[
            pl.BlockSpec((FFN_ROW_TILE, D_MODEL), row),
            _stacked((D_MODEL, D_FF), l, j),
            _stacked((D_MODEL, D_FF), l, j),
            _stacked((D_FF, D_MODEL), l, j),
            pl.BlockSpec((1, D_MODEL), const),
            pl.BlockSpec((1, D_MODEL), const),
        ],
        out_specs=pl.BlockSpec((FFN_ROW_TILE, D_MODEL), row),
        out_shape=jax.ShapeDtypeStruct((t, D_MODEL), F32),
        compiler_params=pltpu.CompilerParams(
            dimension_semantics=("arbitrary",), vmem_limit_bytes=VMEM_LIMIT),
        name="ffn",
    )(x, wg, wu, wd, g, b)


def _rope(x, c, a, bt):
    outs = []
    for h in range(N_HEADS):
        xh = x[:, h * HEAD_W:(h + 1) * HEAD_W]
        up = pltpu.roll(xh, HEAD_W - ROT_DIM // 2, axis=1)
        dn = pltpu.roll(xh, ROT_DIM // 2, axis=1)
        outs.append(xh * c + up * a + dn * bt)
    return outs


def _log_forget(z, lb):
    z2 = z * LOG2E
    log_sig = jnp.minimum(z2, 0.0) - jnp.log2(1.0 + jnp.exp2(-jnp.abs(z2)))
    a = jnp.log2(lb)
    c = jnp.log2(1.0 - lb) + log_sig
    log_f = jnp.maximum(a, c) + jnp.log2(1.0 + jnp.exp2(-jnp.abs(a - c)))
    return jnp.minimum(log_f, 0.0)


def _f8_scale(block_max):
    bits = lax.bitcast_convert_type(F8_BLOCK_MAX / jnp.maximum(block_max, F8_MIN_BLOCK_MAX), jnp.int32)
    return lax.bitcast_convert_type(bits & F32_EXPONENT_MASK, F32)


def _f8_operands(heads, keys, op_ref, inv_ref):
    first = lax.broadcasted_iota(jnp.int32, heads[0].shape, 1) < ATTN_HEAD_DIM
    for h, xh in enumerate(heads):
        swapped = pltpu.roll(xh, ATTN_HEAD_DIM, axis=1)
        for c in range(2):
            x = jnp.where(first, xh, swapped) if c == 0 else jnp.where(first, swapped, xh)
            scale = _f8_scale(jnp.max(jnp.abs(x), axis=(0, 1), keepdims=True))
            x = x * scale
            hi = x.astype(F8).astype(F32)
            lo = x - hi
            if keys:
                op = jnp.concatenate([jnp.where(first, hi, lo), jnp.where(first, hi, 0.0)], axis=1)
            else:
                op = jnp.concatenate([hi, jnp.where(first, lo, 0.0)], axis=1)
            blk = 2 * h + c
            op_ref[:, blk * 2 * HEAD_W:(blk + 1) * 2 * HEAD_W] = op.astype(F8)
            inv_ref[0, blk:blk + 1, :] = jnp.broadcast_to(1.0 / scale, (1, HEAD_W))


def _in_proj_kernel(x_ref, w_ref, wvt_ref, c_ref, a_ref, bt_ref, lbf_ref, lbb_ref,
                    qf_ref, qs_ref, kf_ref, ks_ref, avt_ref, hq_ref, gf_ref, gb_ref, hi_ref, gate_ref):
    xb = x_ref[...].astype(BF16)
    c, a, bt = c_ref[...], a_ref[...], bt_ref[...]

    def proj(j):
        return jnp.dot(xb, w_ref[:, j * 512:(j + 1) * 512], preferred_element_type=F32)

    gf_ref[...] = _log_forget(proj(4), lbf_ref[...])
    gb_ref[...] = _log_forget(proj(5), lbb_ref[...])
    _f8_operands([qh * ATTN_Q_SCALE for qh in _rope(proj(0), c, a, bt)], False, qf_ref, qs_ref)
    _f8_operands(_rope(proj(1), c, a, bt), True, kf_ref, ks_ref)
    hq = proj(3)
    hq_ref[...] = hq * _sigmoid(hq)
    gt = proj(7)
    gate_ref[...] = gt * _sigmoid(gt)
    vt = lax.dot_general(wvt_ref[...], xb, (((1,), (1,)), ((), ())), preferred_element_type=F32).astype(BF16)
    ones = jnp.ones((VT_ROWS - HEAD_W, vt.shape[1]), BF16)
    for h in range(N_HEADS):
        avt_ref[h * VT_ROWS:h * VT_ROWS + HEAD_W, :] = vt[h * HEAD_W:(h + 1) * HEAD_W, :]
        avt_ref[h * VT_ROWS + HEAD_W:(h + 1) * VT_ROWS, :] = ones
    hi_ref[...] = proj(6).astype(BF16)


def _in_proj(x, w_in, wvt, rope_c, rope_a, rope_b, lb_f, lb_b, seq, l):
    t = x.shape[0]
    nseq = seq // ROW_TILE
    row = lambda i: (i, 0)
    const = lambda i: (0, 0)
    pos = lambda i: (i % nseq, 0)
    grp = pl.BlockSpec((ROW_TILE, 512), row)
    shp = lambda dt: jax.ShapeDtypeStruct((t, 512), dt)
    n_blocks = 2 * N_HEADS
    f8_ops = pl.BlockSpec((ROW_TILE, n_blocks * 2 * HEAD_W), row)
    f8_inv = pl.BlockSpec((1, n_blocks, HEAD_W), lambda i: (i, 0, 0))
    f8_ops_shape = jax.ShapeDtypeStruct((t, n_blocks * 2 * HEAD_W), F8)
    f8_inv_shape = jax.ShapeDtypeStruct((t // ROW_TILE, n_blocks, HEAD_W), F32)
    return pl.pallas_call(
        _in_proj_kernel,
        grid=(t // ROW_TILE,),
        in_specs=[
            pl.BlockSpec((ROW_TILE, D_MODEL), row),
            _stacked((D_MODEL, IN_WIDTH), l),
            _stacked((ATTN_WIDTH, D_MODEL), l),
            pl.BlockSpec((ROW_TILE, HEAD_W), pos),
            pl.BlockSpec((ROW_TILE, HEAD_W), pos),
            pl.BlockSpec((ROW_TILE, HEAD_W), pos),
            pl.BlockSpec((1, HG_WIDTH), const),
            pl.BlockSpec((1, HG_WIDTH), const),
        ],
        out_specs=[f8_ops, f8_inv, f8_ops, f8_inv, pl.BlockSpec((N_HEADS * VT_ROWS, ROW_TILE), lambda i: (0, i)),
                   grp, grp, grp, grp, grp],
        out_shape=[f8_ops_shape, f8_inv_shape, f8_ops_shape, f8_inv_shape,
                   jax.ShapeDtypeStruct((N_HEADS * VT_ROWS, t), BF16),
                   shp(F32), shp(F32), shp(F32), shp(BF16), shp(F32)],
        compiler_params=pltpu.CompilerParams(
            dimension_semantics=("arbitrary",), vmem_limit_bytes=VMEM_LIMIT),
        name="in_proj",
    )(x, w_in, wvt, rope_c, rope_a, rope_b, lb_f, lb_b)


def _attn_kernel(lam_ref, qf_ref, qs_ref, kf_ref, ks_ref, vt_ref, g_ref, o_ref, sa_ref, sb_ref, m_ref, acc_ref, *,
                 seq, out_scale):
    head = pl.program_id(1)

    def qblock(i, carry):
        rows = pl.ds(pl.multiple_of(i * ATTN_TQ, ATTN_TQ), ATTN_TQ)
        _attn_qblock(lam_ref, qf_ref.at[rows, :], qs_ref.at[i], kf_ref, ks_ref, head, vt_ref, g_ref,
                     o_ref.at[rows, :], sa_ref, sb_ref, m_ref, acc_ref, seq=seq, out_scale=out_scale)
        return carry

    lax.fori_loop(0, seq // ATTN_TQ, qblock, 0)


def _attn_qblock(lam_ref, qf_ref, qs_ref, kf_ref, ks_ref, head, vt_ref, g_ref, o_ref, sa_ref, sb_ref, m_ref, acc_ref, *,
                 seq, out_scale):
    comp = [slice(c * 2 * HEAD_W, (c + 1) * 2 * HEAD_W) for c in range(2)]
    qf = [qf_ref[:, comp[c]] for c in range(2)]
    q_inv = [qs_ref[pl.ds(2 * head + c, 1), :] for c in range(2)]
    nt = (((1,), (1,)), ((), ()))
    n_chunks = seq // ATTN_TK

    def chunk(j):
        return pl.ds(pl.multiple_of(j * ATTN_TK, ATTN_TK), ATTN_TK)

    def scores(j, s_ref):
        for c in range(2):
            s = lax.dot_general(kf_ref[chunk(j), comp[c]], qf[c], nt, preferred_element_type=F32)
            unscale = (ks_ref[j, pl.ds(2 * head + c, 1), :] * q_inv[c])[:, 0:1]
            s_ref[c] = s.astype(BF16) * unscale.astype(BF16)

    def accumulate(j, s_ref):
        vt = vt_ref[:, chunk(j)]
        for c in range(2):
            s = s_ref[c]
            m = m_ref[c]
            m_new = jnp.maximum(m, jnp.max(s, axis=0, keepdims=True).astype(F32))
            e = jnp.exp2(s - m_new.astype(BF16))
            m_ref[c] = m_new
            acc_ref[c] = jnp.exp2(m - m_new) * acc_ref[c] + jnp.dot(vt, e, preferred_element_type=F32)

    m_ref[...] = jnp.full(m_ref.shape, -jnp.inf, F32)
    acc_ref[...] = jnp.zeros(acc_ref.shape, F32)
    s_bufs = (sa_ref, sb_ref)
    scores(0, s_bufs[0])

    def body(i, carry):
        j0 = ATTN_UNROLL * i
        for u in range(ATTN_UNROLL):
            scores(j0 + u + 1, s_bufs[(u + 1) % 2])
            accumulate(j0 + u, s_bufs[u % 2])
        return carry

    lax.fori_loop(0, n_chunks // ATTN_UNROLL - 1, body, 0)
    j0 = n_chunks - ATTN_UNROLL
    for u in range(ATTN_UNROLL):
        if u + 1 < ATTN_UNROLL:
            scores(j0 + u + 1, s_bufs[(u + 1) % 2])
        accumulate(j0 + u, s_bufs[u % 2])
    lam = lam_ref[0]
    a1, a2 = acc_ref[0], acc_ref[1]
    ot = a1[:HEAD_W, :] / a1[HEAD_W:HEAD_W + 1, :] - lam * (a2[:HEAD_W, :] / a2[HEAD_W:HEAD_W + 1, :])
    ms = jnp.mean(ot * ot, axis=0, keepdims=True)
    ot = ot * lax.rsqrt(ms + EPS) * (g_ref[...] * out_scale)
    o_ref[...] = ot.T.astype(BF16)


def _attn(lam, qf, qs, kf, ks, avt, g, batch, seq, out_scale):
    assert ATTN_TQ == ROW_TILE and ATTN_TK == ROW_TILE
    t = qf.shape[0]
    n_tiles = seq // ROW_TILE
    ops = pl.BlockSpec((seq, 4 * HEAD_W), lambda b, h: (b, h))
    inv = pl.BlockSpec((n_tiles, 2 * N_HEADS, HEAD_W), lambda b, h: (b, 0, 0))
    return pl.pallas_call(
        functools.partial(_attn_kernel, seq=seq, out_scale=out_scale),
        grid=(batch, N_HEADS),
        in_specs=[
            pl.BlockSpec(memory_space=pltpu.SMEM),
            ops, inv, ops, inv,
            pl.BlockSpec((VT_ROWS, seq), lambda b, h: (h, b)),
            pl.BlockSpec((HEAD_W, 1), lambda b, h: (0, 0)),
        ],
        out_specs=pl.BlockSpec((seq, HEAD_W), lambda b, h: (b, h)),
        out_shape=jax.ShapeDtypeStruct((t, ATTN_WIDTH), BF16),
        scratch_shapes=[
            pltpu.VMEM((2, ATTN_TK, ATTN_TQ), BF16),
            pltpu.VMEM((2, ATTN_TK, ATTN_TQ), BF16),
            pltpu.VMEM((2, 1, ATTN_TQ), F32),
            pltpu.VMEM((2, VT_ROWS, ATTN_TQ), F32),
        ],
        compiler_params=pltpu.CompilerParams(
            dimension_semantics=("arbitrary", "arbitrary"), vmem_limit_bytes=VMEM_LIMIT),
        name="attn",
    )(lam, qf, qs, kf, ks, avt, g)


def _cumsum16(g, reverse):
    row = lax.broadcasted_iota(jnp.int32, g.shape, 0)
    b = g
    for sh in (1, 2, 4, 8):
        if reverse:
            b = b + jnp.where(row < HG_STEP - sh, pltpu.roll(b, HG_STEP - sh, axis=0), 0.0)
        else:
            b = b + jnp.where(row >= sh, pltpu.roll(b, sh, axis=0), 0.0)
    return b


def _hgrn_state(q, kk, b, v, st, reverse):
    qd = (q * jnp.exp2(b)).astype(BF16)
    o = lax.dot_general(qd, st.astype(BF16), (((1,), (1,)), ((), ())), preferred_element_type=F32)
    b_end = b[0:1, :] if reverse else b[HG_STEP - 1:HG_STEP, :]
    kd = (kk * jnp.exp2(b_end - b)).astype(BF16)
    upd = lax.dot_general(v, kd, (((0,), (0,)), ((), ())), preferred_element_type=F32)
    return o, st * jnp.exp2(b_end) + upd


def _hgrn_pairwise(q, kk, b, v, reverse):
    c = b - jnp.log2(kk)
    half = HG_STEP // 2
    row = lax.broadcasted_iota(jnp.int32, (half, HEAD_W), 0)
    vf = v.astype(F32)
    o = [jnp.zeros((half, HEAD_W), F32), jnp.zeros((half, HEAD_W), F32)]
    for s in range(HG_STEP):
        cs = c[s:s + 1, :]
        vs = vf[s:s + 1, :]
        for hi in range(2):
            r0 = hi * half
            if (r0 > s) if reverse else (r0 + half - 1 < s):
                continue
            d = b[r0:r0 + half, :] - cs
            if (r0 + half - 1 > s) if reverse else (r0 < s):
                seen = (row + r0 <= s) if reverse else (row + r0 >= s)
                d = jnp.where(seen, d, -jnp.inf)
            col = jnp.sum(q[r0:r0 + half, :] * jnp.exp2(d), axis=1, keepdims=True)
            o[hi] = o[hi] + col * vs
    return jnp.concatenate(o, axis=0)


def _hgrn_kernel(qf_ref, gf_ref, vf_ref, qb_ref, gb_ref, vb_ref, of_ref, ob_ref, sf_ref, sb_ref):
    @pl.when(pl.program_id(1) == 0)
    def _():
        sf_ref[...] = jnp.zeros_like(sf_ref)
        sb_ref[...] = jnp.zeros_like(sb_ref)

    nstep = HG_BLOCK // HG_STEP

    def body(i, carry):
        rf = pl.multiple_of(i * HG_STEP, HG_STEP)
        rb = pl.multiple_of((nstep - 1 - i) * HG_STEP, HG_STEP)
        chains = []
        for h in range(N_HEADS):
            cols = slice(h * HEAD_W, (h + 1) * HEAD_W)
            chains.append((qf_ref, gf_ref, vf_ref, of_ref, sf_ref, rf, h, cols, False))
            chains.append((qb_ref, gb_ref, vb_ref, ob_ref, sb_ref, rb, h, cols, True))
        live = []
        for q_ref, g_ref, v_ref, o_ref, s_ref, r, h, cols, reverse in chains:
            rows = pl.ds(r, HG_STEP)
            q, g, v = q_ref[rows, cols], g_ref[rows, cols], v_ref[rows, cols]
            kk = 1.0 - jnp.exp2(g)
            b = _cumsum16(g, reverse)
            o, st = _hgrn_state(q, kk, b, v, s_ref[h], reverse)
            o_ref[rows, cols] = o
            s_ref[h] = st
            live.append((q, kk, b, v))
        for (q_ref, g_ref, v_ref, o_ref, s_ref, r, h, cols, reverse), (q, kk, b, v) in zip(chains, live):
            rows = pl.ds(r, HG_STEP)
            o_ref[rows, cols] += _hgrn_pairwise(q, kk, b, v, reverse)
        return carry

    lax.fori_loop(0, nstep, body, 0)


def _hgrn(hq, gf, gb, hi, batch, seq):
    t = hq.shape[0]
    nb = seq // HG_BLOCK
    fwd = pl.BlockSpec((HG_BLOCK, HG_WIDTH), lambda b, j: (b * nb + j, 0))
    bwd = pl.BlockSpec((HG_BLOCK, HG_WIDTH), lambda b, j: (b * nb + nb - 1 - j, 0))
    return pl.pallas_call(
        _hgrn_kernel,
        grid=(batch, nb),
        in_specs=[fwd, fwd, fwd, bwd, bwd, bwd],
        out_specs=[fwd, bwd],
        out_shape=[jax.ShapeDtypeStruct((t, HG_WIDTH), F32)] * 2,
        scratch_shapes=[pltpu.VMEM((N_HEADS, HEAD_W, HEAD_W), F32)] * 2,
        compiler_params=pltpu.CompilerParams(
            dimension_semantics=("arbitrary", "arbitrary"), vmem_limit_bytes=VMEM_LIMIT),
        name="hgrn",
    )(hq, gf, hi, hq, gb, hi)


def _out_proj_ffn_kernel(x_ref, ao_ref, of_ref, ob_ref, gate_ref, hgg_ref, w_ref, g1_ref, b1_ref,
                         wg_ref, wu_ref, wd_ref, g2_ref, b2_ref, o_ref):
    x = x_ref[...]
    ho = of_ref[...] + ob_ref[...]
    gate = gate_ref[...]
    hgg = hgg_ref[...]
    mix = jnp.dot(ao_ref[...], w_ref[0:ATTN_WIDTH, :], preferred_element_type=F32)
    for h in range(N_HEADS):
        cols = slice(h * HEAD_W, (h + 1) * HEAD_W)
        oh = ho[:, cols]
        ms = jnp.mean(oh * oh, axis=-1, keepdims=True)
        nh = (oh * lax.rsqrt(ms + EPS) * hgg * gate[:, cols]).astype(BF16)
        mix = mix + jnp.dot(nh, w_ref[ATTN_WIDTH + h * HEAD_W:ATTN_WIDTH + (h + 1) * HEAD_W, :],
                            preferred_element_type=F32)
    x = _layer_norm(ALPHA * x + mix, g1_ref[...], b1_ref[...])
    o_ref[...] = _ffn_block(x, wg_ref, wu_ref, wd_ref, g2_ref[...], b2_ref[...])


def _out_proj_ffn(x, ao, of, ob, gate, hgg, w_out, g1, b1, wg, wu, wd, g2, b2, l):
    t = x.shape[0]
    row = lambda i: (i, 0)
    const = lambda i: (0, 0)
    grp = pl.BlockSpec((ROW_TILE, 512), row)
    vec = pl.BlockSpec((1, D_MODEL), const)
    return pl.pallas_call(
        _out_proj_ffn_kernel,
        grid=(t // ROW_TILE,),
        in_specs=[
            pl.BlockSpec((ROW_TILE, D_MODEL), row),
            grp, grp, grp, grp,
            pl.BlockSpec((1, HEAD_W), const),
            _stacked((D_MODEL, D_MODEL), l),
            vec, vec,
            _stacked((D_MODEL, D_FF), l, 1),
            _stacked((D_MODEL, D_FF), l, 1),
            _stacked((D_FF, D_MODEL), l, 1),
            vec, vec,
        ],
        out_specs=pl.BlockSpec((ROW_TILE, D_MODEL), row),
        out_shape=jax.ShapeDtypeStruct((t, D_MODEL), F32),
        compiler_params=pltpu.CompilerParams(
            dimension_semantics=("arbitrary",), vmem_limit_bytes=VMEM_LIMIT),
        name="out_proj_ffn",
    )(x, ao, of, ob, gate, hgg, w_out, g1, b1, wg, wu, wd, g2, b2)


def _rope_tables(seq):
    half = ROT_DIM // 2
    lane = jnp.arange(HEAD_W) % ATTN_HEAD_DIM
    inv_freq = jnp.float32(ROPE_THETA) ** (-(lane % half).astype(F32) * 2.0 / ROT_DIM)
    ang = jnp.arange(seq, dtype=F32)[:, None] * inv_freq[None, :]
    cos, sin = jnp.cos(ang), jnp.sin(ang)
    rope_c = jnp.where(lane < ROT_DIM, cos, 1.0)
    rope_a = jnp.where(lane < half, -sin, 0.0)
    rope_b = jnp.where((lane >= half) & (lane < ROT_DIM), sin, 0.0)
    return rope_c, rope_a, rope_b


def _trunk(x3, p):
    batch, seq, _ = x3.shape
    x = x3.reshape(batch * seq, D_MODEL)
    rope_c, rope_a, rope_b = _rope_tables(seq)
    for l in range(DEPTH):
        lam_init = 0.8 - 0.6 * math.exp(-0.3 * l)
        x = _ffn(x, p["wg"], p["wu"], p["wd"], p["ln_g"][l][0], p["ln_b"][l][0], l, 0)
        qf, qs, kf, ks, avt, hq, gf, gb, hi, gate = _in_proj(
            x, p["w_in"], p["wvt"], rope_c, rope_a, rope_b, p["lbs"][0][l], p["lbs"][1][l], seq, l)
        ao = _attn(p["lam"][l], qf, qs, kf, ks, avt, p["attn_g"][l], batch, seq, 1.0 - lam_init)
        of, ob = _hgrn(hq, gf, gb, hi, batch, seq)
        x = _out_proj_ffn(x, ao, of, ob, gate, p["hg_g"][l], p["w_out"], p["ln_g"][l][1], p["ln_b"][l][1],
                          p["wg"], p["wu"], p["wd"], p["ln_g"][l][2], p["ln_b"][l][2], l)
    return x.reshape(batch, seq, D_MODEL)


def kernel(x_prompt, x_sample, w_in, w_out, attn_lambda, attn_norm_g, hg_norm_g, hg_lower_bound,
           ffn_w_gate, ffn_w_up, ffn_w_down, ln_g, ln_b):
    sm = jax.nn.softmax(hg_lower_bound.astype(F32), axis=1)
    lbs = jnp.maximum(jnp.cumsum(sm, axis=1) - sm[:, :1], 0.0)
    lp = attn_lambda.astype(F32)
    lam_init = jnp.asarray([0.8 - 0.6 * math.exp(-0.3 * l) for l in range(DEPTH)], F32)
    lam = (jnp.exp(jnp.sum(lp[:, 0] * lp[:, 1], axis=-1))
           - jnp.exp(jnp.sum(lp[:, 2] * lp[:, 3], axis=-1)) + lam_init)
    p = {
        "w_in": w_in.astype(BF16), "w_out": w_out.astype(BF16),
        "wvt": jnp.swapaxes(w_in[:, :, 2 * ATTN_WIDTH:3 * ATTN_WIDTH], 1, 2).astype(BF16),
        "wg": ffn_w_gate.astype(BF16), "wu": ffn_w_up.astype(BF16), "wd": ffn_w_down.astype(BF16),
        "ln_g": ln_g.reshape(DEPTH, 3, 1, D_MODEL), "ln_b": ln_b.reshape(DEPTH, 3, 1, D_MODEL),
        "attn_g": attn_norm_g.reshape(DEPTH, HEAD_W, 1), "hg_g": hg_norm_g.reshape(DEPTH, 1, HEAD_W),
        "lbs": lbs.reshape(2, DEPTH, 1, HG_WIDTH), "lam": lam.reshape(DEPTH, 1),
    }
    return (_trunk(x_prompt, p), _trunk(x_sample, p))
```

```python
import functools
import math

import jax
import jax.numpy as jnp
from jax import lax
from jax.experimental import pallas as pl
from jax.experimental.pallas import tpu as pltpu

D_MODEL = 1024
DEPTH = 2
ATTN_WIDTH = 512
HG_WIDTH = 512
HEAD_W = 128
N_HEADS = 4
ATTN_HEAD_DIM = 64
ROT_DIM = 16
ROPE_THETA = 500000.0
D_FF = 2816
ALPHA = (2 * DEPTH) ** 0.25
EPS = 1e-5
IN_WIDTH = 3 * ATTN_WIDTH + 5 * HG_WIDTH
LOG2E = math.log2(math.e)
ATTN_Q_SCALE = ATTN_HEAD_DIM ** -0.5 * LOG2E

ROW_TILE = 512
FFN_ROW_TILE = 2 * ROW_TILE
FF_CHUNK = 512
ATTN_TQ = 1024
ATTN_UNROLL = 2
ATTN_TK = 2048
VT_ROWS = HEAD_W + 16
HG_BLOCK = 512
HG_STEP = 16
VMEM_LIMIT = 56 * 1024 * 1024

BF16 = jnp.bfloat16
F32 = jnp.float32
F8 = jnp.float8_e4m3fn
F8_BLOCK_MAX = 128.0
F32_EXPONENT_MASK = 0x7F800000
F8_MIN_BLOCK_MAX = 1e-30


def _layer_norm(y, g, b):
    mu = jnp.mean(y, axis=-1, keepdims=True)
    yc = y - mu
    var = jnp.mean(yc * yc, axis=-1, keepdims=True)
    return yc * lax.rsqrt(var + EPS) * g + b


def _sigmoid(z):
    return 1.0 / (1.0 + jnp.exp(-z))


def _ffn_block(x, wg_ref, wu_ref, wd_ref, g, b):
    xb = x.astype(BF16)
    acc = jnp.zeros((x.shape[0], D_MODEL), F32)
    for f0 in range(0, D_FF, FF_CHUNK):
        f1 = min(f0 + FF_CHUNK, D_FF)
        hg = jnp.dot(xb, wg_ref[:, f0:f1], preferred_element_type=F32)
        hu = jnp.dot(xb, wu_ref[:, f0:f1], preferred_element_type=F32)
        h = (hg * _sigmoid(hg) * hu).astype(BF16)
        acc = acc + jnp.dot(h, wd_ref[f0:f1, :], preferred_element_type=F32)
    return _layer_norm(ALPHA * x + 0.5 * acc, g, b)


def _ffn_kernel(x_ref, wg_ref, wu_ref, wd_ref, g_ref, b_ref, o_ref):
    for r in range(0, FFN_ROW_TILE, ROW_TILE):
        o_ref[r:r + ROW_TILE, :] = _ffn_block(x_ref[r:r + ROW_TILE, :], wg_ref, wu_ref, wd_ref,
                                              g_ref[...], b_ref[...])


def _stacked(shape, *lead):
    return pl.BlockSpec((None,) * len(lead) + shape, lambda i: lead + (0,) * len(shape),
                        pipeline_mode=pl.Buffered(1))


def _ffn(x, wg, wu, wd, g, b, l, j):
    t = x.shape[0]
    row = lambda i: (i, 0)
    const = lambda i: (0, 0)
    return pl.pallas_call(
        _ffn_kernel,
        grid=(t // FFN_ROW_TILE,),
        in_specs=[
            pl.BlockSpec((FFN_ROW_TILE, D_MODEL), row),
            _stacked((D_MODEL, D_FF), l, j),
            _stacked((D_MODEL, D_FF), l, j),
            _stacked((D_FF, D_MODEL), l, j),
            pl.BlockSpec((1, D_MODEL), const),
            pl.BlockSpec((1, D_MODEL), const),
        ],
        out_specs=pl.BlockSpec((FFN_ROW_TILE, D_MODEL), row),
        out_shape=jax.ShapeDtypeStruct((t, D_MODEL), F32),
        compiler_params=pltpu.CompilerParams(
            dimension_semantics=("arbitrary",), vmem_limit_bytes=VMEM_LIMIT),
        name="ffn",
    )(x, wg, wu, wd, g, b)


def _rope(x, c, a, bt):
    outs = []
    for h in range(N_HEADS):
        xh = x[:, h * HEAD_W:(h + 1) * HEAD_W]
        up = pltpu.roll(xh, HEAD_W - ROT_DIM // 2, axis=1)
        dn = pltpu.roll(xh, ROT_DIM // 2, axis=1)
        outs.append(xh * c + up * a + dn * bt)
    return outs


def _log_forget(z, lb):
    z2 = z * LOG2E
    log_sig = jnp.minimum(z2, 0.0) - jnp.log2(1.0 + jnp.exp2(-jnp.abs(z2)))
    a = jnp.log2(lb)
    c = jnp.log2(1.0 - lb) + log_sig
    log_f = jnp.maximum(a, c) + jnp.log2(1.0 + jnp.exp2(-jnp.abs(a - c)))
    return jnp.minimum(log_f, 0.0)


def _in_proj_kernel(x_ref, w_ref, wvt_ref, c_ref, a_ref, bt_ref, lbf_ref, lbb_ref,
                    aq_ref, ak_ref, avt_ref, hq_ref, gf_ref, gb_ref, hi_ref, gate_ref):
    xb = x_ref[...].astype(BF16)
    c, a, bt = c_ref[...], a_ref[...], bt_ref[...]

    def proj(j):
        return jnp.dot(xb, w_ref[:, j * 512:(j + 1) * 512], preferred_element_type=F32)

    gf_ref[...] = _log_forget(proj(4), lbf_ref[...])
    gb_ref[...] = _log_forget(proj(5), lbb_ref[...])
    q = _rope(proj(0), c, a, bt)
    for h in range(N_HEADS):
        aq_ref[:, h * HEAD_W:(h + 1) * HEAD_W] = (q[h] * ATTN_Q_SCALE).astype(BF16)
    k = _rope(proj(1), c, a, bt)
    for h in range(N_HEADS):
        ak_ref[:, h * HEAD_W:(h + 1) * HEAD_W] = k[h].astype(BF16)
    hq = proj(3)
    hq_ref[...] = hq * _sigmoid(hq)
    gt = proj(7)
    gate_ref[...] = gt * _sigmoid(gt)
    vt = lax.dot_general(wvt_ref[...], xb, (((1,), (1,)), ((), ())), preferred_element_type=F32).astype(BF16)
    ones = jnp.ones((VT_ROWS - HEAD_W, vt.shape[1]), BF16)
    for h in range(N_HEADS):
        avt_ref[h * VT_ROWS:h * VT_ROWS + HEAD_W, :] = vt[h * HEAD_W:(h + 1) * HEAD_W, :]
        avt_ref[h * VT_ROWS + HEAD_W:(h + 1) * VT_ROWS, :] = ones
    hi_ref[...] = proj(6).astype(BF16)


def _in_proj(x, w_in, wvt, rope_c, rope_a, rope_b, lb_f, lb_b, seq, l):
    t = x.shape[0]
    nseq = seq // ROW_TILE
    row = lambda i: (i, 0)
    const = lambda i: (0, 0)
    pos = lambda i: (i % nseq, 0)
    grp = pl.BlockSpec((ROW_TILE, 512), row)
    shp = lambda dt: jax.ShapeDtypeStruct((t, 512), dt)
    return pl.pallas_call(
        _in_proj_kernel,
        grid=(t // ROW_TILE,),
        in_specs=[
            pl.BlockSpec((ROW_TILE, D_MODEL), row),
            _stacked((D_MODEL, IN_WIDTH), l),
            _stacked((ATTN_WIDTH, D_MODEL), l),
            pl.BlockSpec((ROW_TILE, HEAD_W), pos),
            pl.BlockSpec((ROW_TILE, HEAD_W), pos),
            pl.BlockSpec((ROW_TILE, HEAD_W), pos),
            pl.BlockSpec((1, HG_WIDTH), const),
            pl.BlockSpec((1, HG_WIDTH), const),
        ],
        out_specs=[grp, grp, pl.BlockSpec((N_HEADS * VT_ROWS, ROW_TILE), lambda i: (0, i)),
                   grp, grp, grp, grp, grp],
        out_shape=[shp(BF16), shp(BF16), jax.ShapeDtypeStruct((N_HEADS * VT_ROWS, t), BF16),
                   shp(F32), shp(F32), shp(F32), shp(BF16), shp(F32)],
        compiler_params=pltpu.CompilerParams(
            dimension_semantics=("arbitrary",), vmem_limit_bytes=VMEM_LIMIT),
        name="in_proj",
    )(x, w_in, wvt, rope_c, rope_a, rope_b, lb_f, lb_b)


def _component_max(a):
    first = lax.broadcasted_iota(jnp.int32, a.shape, 1) < ATTN_HEAD_DIM
    m0 = jnp.max(jnp.where(first, a, 0.0), axis=1, keepdims=True)
    m1 = jnp.max(jnp.where(first, 0.0, a), axis=1, keepdims=True)
    return jnp.maximum(jnp.where(first, m0, m1), F8_MIN_BLOCK_MAX)


def _f8_scale(block_max):
    bits = lax.bitcast_convert_type(F8_BLOCK_MAX / block_max, jnp.int32)
    return lax.bitcast_convert_type(bits & F32_EXPONENT_MASK, F32)


def _f8_features(x, keys):
    hi = x.astype(F8).astype(F32)
    lo = x - hi
    first = lax.broadcasted_iota(jnp.int32, x.shape, 1) < ATTN_HEAD_DIM
    hi_sw = pltpu.roll(hi, ATTN_HEAD_DIM, axis=1)
    lo_sw = pltpu.roll(lo, ATTN_HEAD_DIM, axis=1)
    zero = jnp.zeros_like(x)
    if keys:
        groups = ((jnp.where(first, hi, lo_sw), jnp.where(first, hi, zero)),
                  (jnp.where(first, hi_sw, lo), jnp.where(first, hi_sw, zero)))
    else:
        groups = ((jnp.where(first, hi, hi_sw), jnp.where(first, lo, zero)),
                  (jnp.where(first, hi_sw, hi), jnp.where(first, lo_sw, zero)))
    return [jnp.concatenate(g, axis=1).astype(F8) for g in groups]


def _attn_kernel(lam_ref, q_ref, k_ref, vt_ref, g_ref, o_ref, kf_ref, sa_ref, sb_ref, m_ref, acc_ref, *,
                 seq, out_scale):
    n_chunks = seq // ATTN_TK

    def chunk(j):
        return pl.ds(pl.multiple_of(j * ATTN_TK, ATTN_TK), ATTN_TK)

    def kmax_body(j, a):
        return jnp.maximum(a, jnp.max(jnp.abs(k_ref[chunk(j), :].astype(F32)), axis=0, keepdims=True))

    kscale = _f8_scale(_component_max(lax.fori_loop(0, n_chunks, kmax_body, jnp.zeros((1, HEAD_W), F32))))

    def kfeat_body(j, carry):
        feats = _f8_features(k_ref[chunk(j), :].astype(F32) * kscale, keys=True)
        for c in range(2):
            kf_ref[c, chunk(j), :] = feats[c]
        return carry

    lax.fori_loop(0, n_chunks, kfeat_body, 0)

    def qblock(i, carry):
        rows = pl.ds(pl.multiple_of(i * ATTN_TQ, ATTN_TQ), ATTN_TQ)
        _attn_qblock(lam_ref, q_ref[rows, :], kscale, kf_ref, vt_ref, g_ref, o_ref.at[rows, :],
                     sa_ref, sb_ref, m_ref, acc_ref, seq=seq, out_scale=out_scale)
        return carry

    lax.fori_loop(0, seq // ATTN_TQ, qblock, 0)


def _attn_qblock(lam_ref, q, kscale, kf_ref, vt_ref, g_ref, o_ref, sa_ref, sb_ref, m_ref, acc_ref, *, seq, out_scale):
    qf32 = q.astype(F32)
    qscale = _f8_scale(_component_max(jnp.max(jnp.abs(qf32), axis=0, keepdims=True)))
    qf = _f8_features(qf32 * qscale, keys=False)
    unscale = ((1.0 / kscale) * (1.0 / qscale)).astype(BF16)
    unscale = (unscale[:, 0:1], unscale[:, HEAD_W - 1:HEAD_W])
    nt = (((1,), (1,)), ((), ()))
    n_chunks = seq // ATTN_TK

    def chunk(j):
        return pl.ds(pl.multiple_of(j * ATTN_TK, ATTN_TK), ATTN_TK)

    def scores(j, s_ref):
        for c in range(2):
            s = lax.dot_general(kf_ref[c, chunk(j), :], qf[c], nt, preferred_element_type=F32)
            s_ref[c] = s.astype(BF16) * unscale[c]

    def accumulate(j, s_ref):
        vt = vt_ref[:, chunk(j)]
        for c in range(2):
            s = s_ref[c]
            m = m_ref[c]
            m_new = jnp.maximum(m, jnp.max(s, axis=0, keepdims=True).astype(F32))
            e = jnp.exp2(s - m_new.astype(BF16))
            m_ref[c] = m_new
            acc_ref[c] = jnp.exp2(m - m_new) * acc_ref[c] + jnp.dot(vt, e, preferred_element_type=F32)

    m_ref[...] = jnp.full(m_ref.shape, -jnp.inf, F32)
    acc_ref[...] = jnp.zeros(acc_ref.shape, F32)
    s_bufs = (sa_ref, sb_ref)
    scores(0, s_bufs[0])

    unroll = min(ATTN_UNROLL, n_chunks)

    def body(i, carry):
        j0 = unroll * i
        for u in range(unroll):
            scores(j0 + u + 1, s_bufs[(u + 1) % 2])
            accumulate(j0 + u, s_bufs[u % 2])
        return carry

    lax.fori_loop(0, n_chunks // unroll - 1, body, 0)
    j0 = n_chunks - unroll
    for u in range(unroll):
        if u + 1 < unroll:
            scores(j0 + u + 1, s_bufs[(u + 1) % 2])
        accumulate(j0 + u, s_bufs[u % 2])
    lam = lam_ref[0]
    a1, a2 = acc_ref[0], acc_ref[1]
    ot = a1[:HEAD_W, :] / a1[HEAD_W:HEAD_W + 1, :] - lam * (a2[:HEAD_W, :] / a2[HEAD_W:HEAD_W + 1, :])
    ms = jnp.mean(ot * ot, axis=0, keepdims=True)
    ot = ot * lax.rsqrt(ms + EPS) * (g_ref[...] * out_scale)
    o_ref[...] = ot.T.astype(BF16)


def _attn(lam, aq, ak, avt, g, batch, seq, out_scale):
    t = aq.shape[0]
    return pl.pallas_call(
        functools.partial(_attn_kernel, seq=seq, out_scale=out_scale),
        grid=(batch, N_HEADS),
        in_specs=[
            pl.BlockSpec(memory_space=pltpu.SMEM),
            pl.BlockSpec((seq, HEAD_W), lambda b, h: (b, h)),
            pl.BlockSpec((seq, HEAD_W), lambda b, h: (b, h)),
            pl.BlockSpec((VT_ROWS, seq), lambda b, h: (h, b)),
            pl.BlockSpec((HEAD_W, 1), lambda b, h: (0, 0)),
        ],
        out_specs=pl.BlockSpec((seq, HEAD_W), lambda b, h: (b, h)),
        out_shape=jax.ShapeDtypeStruct((t, ATTN_WIDTH), BF16),
        scratch_shapes=[
            pltpu.VMEM((2, seq, 2 * HEAD_W), F8),
            pltpu.VMEM((2, ATTN_TK, ATTN_TQ), BF16),
            pltpu.VMEM((2, ATTN_TK, ATTN_TQ), BF16),
            pltpu.VMEM((2, 1, ATTN_TQ), F32),
            pltpu.VMEM((2, VT_ROWS, ATTN_TQ), F32),
        ],
        compiler_params=pltpu.CompilerParams(
            dimension_semantics=("arbitrary", "arbitrary"), vmem_limit_bytes=VMEM_LIMIT),
        name="attn",
    )(lam, aq, ak, avt, g)


def _cumsum16(g, reverse):
    row = lax.broadcasted_iota(jnp.int32, g.shape, 0)
    b = g
    for sh in (1, 2, 4, 8):
        if reverse:
            b = b + jnp.where(row < HG_STEP - sh, pltpu.roll(b, HG_STEP - sh, axis=0), 0.0)
        else:
            b = b + jnp.where(row >= sh, pltpu.roll(b, sh, axis=0), 0.0)
    return b


def _hgrn_state(q, kk, b, v, st, reverse):
    qd = (q * jnp.exp2(b)).astype(BF16)
    o = lax.dot_general(qd, st.astype(BF16), (((1,), (1,)), ((), ())), preferred_element_type=F32)
    b_end = b[0:1, :] if reverse else b[HG_STEP - 1:HG_STEP, :]
    kd = (kk * jnp.exp2(b_end - b)).astype(BF16)
    upd = lax.dot_general(v, kd, (((0,), (0,)), ((), ())), preferred_element_type=F32)
    return o, st * jnp.exp2(b_end) + upd


def _hgrn_pairwise(q, kk, b, v, reverse):
    c = b - jnp.log2(kk)
    half = HG_STEP // 2
    row = lax.broadcasted_iota(jnp.int32, (half, HEAD_W), 0)
    vf = v.astype(F32)
    o = [jnp.zeros((half, HEAD_W), F32), jnp.zeros((half, HEAD_W), F32)]
    for s in range(HG_STEP):
        cs = c[s:s + 1, :]
        vs = vf[s:s + 1, :]
        for hi in range(2):
            r0 = hi * half
            if (r0 > s) if reverse else (r0 + half - 1 < s):
                continue
            d = b[r0:r0 + half, :] - cs
            if (r0 + half - 1 > s) if reverse else (r0 < s):
                seen = (row + r0 <= s) if reverse else (row + r0 >= s)
                d = jnp.where(seen, d, -jnp.inf)
            col = jnp.sum(q[r0:r0 + half, :] * jnp.exp2(d), axis=1, keepdims=True)
            o[hi] = o[hi] + col * vs
    return jnp.concatenate(o, axis=0)


def _hgrn_kernel(qf_ref, gf_ref, vf_ref, qb_ref, gb_ref, vb_ref, of_ref, ob_ref, sf_ref, sb_ref):
    @pl.when(pl.program_id(1) == 0)
    def _():
        sf_ref[...] = jnp.zeros_like(sf_ref)
        sb_ref[...] = jnp.zeros_like(sb_ref)

    nstep = HG_BLOCK // HG_STEP

    def body(i, carry):
        rf = pl.multiple_of(i * HG_STEP, HG_STEP)
        rb = pl.multiple_of((nstep - 1 - i) * HG_STEP, HG_STEP)
        chains = []
        for h in range(N_HEADS):
            cols = slice(h * HEAD_W, (h + 1) * HEAD_W)
            chains.append((qf_ref, gf_ref, vf_ref, of_ref, sf_ref, rf, h, cols, False))
            chains.append((qb_ref, gb_ref, vb_ref, ob_ref, sb_ref, rb, h, cols, True))
        live = []
        for q_ref, g_ref, v_ref, o_ref, s_ref, r, h, cols, reverse in chains:
            rows = pl.ds(r, HG_STEP)
            q, g, v = q_ref[rows, cols], g_ref[rows, cols], v_ref[rows, cols]
            kk = 1.0 - jnp.exp2(g)
            b = _cumsum16(g, reverse)
            o, st = _hgrn_state(q, kk, b, v, s_ref[h], reverse)
            o_ref[rows, cols] = o
            s_ref[h] = st
            live.append((q, kk, b, v))
        for (q_ref, g_ref, v_ref, o_ref, s_ref, r, h, cols, reverse), (q, kk, b, v) in zip(chains, live):
            rows = pl.ds(r, HG_STEP)
            o_ref[rows, cols] += _hgrn_pairwise(q, kk, b, v, reverse)
        return carry

    lax.fori_loop(0, nstep, body, 0)


def _hgrn(hq, gf, gb, hi, batch, seq):
    t = hq.shape[0]
    nb = seq // HG_BLOCK
    fwd = pl.BlockSpec((HG_BLOCK, HG_WIDTH), lambda b, j: (b * nb + j, 0))
    bwd = pl.BlockSpec((HG_BLOCK, HG_WIDTH), lambda b, j: (b * nb + nb - 1 - j, 0))
    return pl.pallas_call(
        _hgrn_kernel,
        grid=(batch, nb),
        in_specs=[fwd, fwd, fwd, bwd, bwd, bwd],
        out_specs=[fwd, bwd],
        out_shape=[jax.ShapeDtypeStruct((t, HG_WIDTH), F32)] * 2,
        scratch_shapes=[pltpu.VMEM((N_HEADS, HEAD_W, HEAD_W), F32)] * 2,
        compiler_params=pltpu.CompilerParams(
            dimension_semantics=("arbitrary", "arbitrary"), vmem_limit_bytes=VMEM_LIMIT),
        name="hgrn",
    )(hq, gf, hi, hq, gb, hi)


def _out_proj_ffn_kernel(x_ref, ao_ref, of_ref, ob_ref, gate_ref, hgg_ref, w_ref, g1_ref, b1_ref,
                         wg_ref, wu_ref, wd_ref, g2_ref, b2_ref, o_ref):
    x = x_ref[...]
    ho = of_ref[...] + ob_ref[...]
    gate = gate_ref[...]
    hgg = hgg_ref[...]
    mix = jnp.dot(ao_ref[...], w_ref[0:ATTN_WIDTH, :], preferred_element_type=F32)
    for h in range(N_HEADS):
        cols = slice(h * HEAD_W, (h + 1) * HEAD_W)
        oh = ho[:, cols]
        ms = jnp.mean(oh * oh, axis=-1, keepdims=True)
        nh = (oh * lax.rsqrt(ms + EPS) * hgg * gate[:, cols]).astype(BF16)
        mix = mix + jnp.dot(nh, w_ref[ATTN_WIDTH + h * HEAD_W:ATTN_WIDTH + (h + 1) * HEAD_W, :],
                            preferred_element_type=F32)
    x = _layer_norm(ALPHA * x + mix, g1_ref[...], b1_ref[...])
    o_ref[...] = _ffn_block(x, wg_ref, wu_ref, wd_ref, g2_ref[...], b2_ref[...])


def _out_proj_ffn(x, ao, of, ob, gate, hgg, w_out, g1, b1, wg, wu, wd, g2, b2, l):
    t = x.shape[0]
    row = lambda i: (i, 0)
    const = lambda i: (0, 0)
    grp = pl.BlockSpec((ROW_TILE, 512), row)
    vec = pl.BlockSpec((1, D_MODEL), const)
    return pl.pallas_call(
        _out_proj_ffn_kernel,
        grid=(t // ROW_TILE,),
        in_specs=[
            pl.BlockSpec((ROW_TILE, D_MODEL), row),
            grp, grp, grp, grp,
            pl.BlockSpec((1, HEAD_W), const),
            _stacked((D_MODEL, D_MODEL), l),
            vec, vec,
            _stacked((D_MODEL, D_FF), l, 1),
            _stacked((D_MODEL, D_FF), l, 1),
            _stacked((D_FF, D_MODEL), l, 1),
            vec, vec,
        ],
        out_specs=pl.BlockSpec((ROW_TILE, D_MODEL), row),
        out_shape=jax.ShapeDtypeStruct((t, D_MODEL), F32),
        compiler_params=pltpu.CompilerParams(
            dimension_semantics=("arbitrary",), vmem_limit_bytes=VMEM_LIMIT),
        name="out_proj_ffn",
    )(x, ao, of, ob, gate, hgg, w_out, g1, b1, wg, wu, wd, g2, b2)


def _rope_tables(seq):
    half = ROT_DIM // 2
    lane = jnp.arange(HEAD_W) % ATTN_HEAD_DIM
    inv_freq = jnp.float32(ROPE_THETA) ** (-(lane % half).astype(F32) * 2.0 / ROT_DIM)
    ang = jnp.arange(seq, dtype=F32)[:, None] * inv_freq[None, :]
    cos, sin = jnp.cos(ang), jnp.sin(ang)
    rope_c = jnp.where(lane < ROT_DIM, cos, 1.0)
    rope_a = jnp.where(lane < half, -sin, 0.0)
    rope_b = jnp.where((lane >= half) & (lane < ROT_DIM), sin, 0.0)
    return rope_c, rope_a, rope_b


def _trunk(x3, p):
    batch, seq, _ = x3.shape
    x = x3.reshape(batch * seq, D_MODEL)
    rope_c, rope_a, rope_b = _rope_tables(seq)
    for l in range(DEPTH):
        lam_init = 0.8 - 0.6 * math.exp(-0.3 * l)
        x = _ffn(x, p["wg"], p["wu"], p["wd"], p["ln_g"][l][0], p["ln_b"][l][0], l, 0)
        aq, ak, avt, hq, gf, gb, hi, gate = _in_proj(
            x, p["w_in"], p["wvt"], rope_c, rope_a, rope_b, p["lbs"][0][l], p["lbs"][1][l], seq, l)
        ao = _attn(p["lam"][l], aq, ak, avt, p["attn_g"][l], batch, seq, 1.0 - lam_init)
        of, ob = _hgrn(hq, gf, gb, hi, batch, seq)
        x = _out_proj_ffn(x, ao, of, ob, gate, p["hg_g"][l], p["w_out"], p["ln_g"][l][1], p["ln_b"][l][1],
                          p["wg"], p["wu"], p["wd"], p["ln_g"][l][2], p["ln_b"][l][2], l)
    return x.reshape(batch, seq, D_MODEL)


def kernel(x_prompt, x_sample, w_in, w_out, attn_lambda, attn_norm_g, hg_norm_g, hg_lower_bound,
           ffn_w_gate, ffn_w_up, ffn_w_down, ln_g, ln_b):
    sm = jax.nn.softmax(hg_lower_bound.astype(F32), axis=1)
    lbs = jnp.maximum(jnp.cumsum(sm, axis=1) - sm[:, :1], 0.0)
    lp = attn_lambda.astype(F32)
    lam_init = jnp.asarray([0.8 - 0.6 * math.exp(-0.3 * l) for l in range(DEPTH)], F32)
    lam = (jnp.exp(jnp.sum(lp[:, 0] * lp[:, 1], axis=-1))
           - jnp.exp(jnp.sum(lp[:, 2] * lp[:, 3], axis=-1)) + lam_init)
    p = {
        "w_in": w_in.astype(BF16), "w_out": w_out.astype(BF16),
        "wvt": jnp.swapaxes(w_in[:, :, 2 * ATTN_WIDTH:3 * ATTN_WIDTH], 1, 2).astype(BF16),
        "wg": ffn_w_gate.astype(BF16), "wu": ffn_w_up.astype(BF16), "wd": ffn_w_down.astype(BF16),
        "ln_g": ln_g.reshape(DEPTH, 3, 1, D_MODEL), "ln_b": ln_b.reshape(DEPTH, 3, 1, D_MODEL),
        "attn_g": attn_norm_g.reshape(DEPTH, HEAD_W, 1), "hg_g": hg_norm_g.reshape(DEPTH, 1, HEAD_W),
        "lbs": lbs.reshape(2, DEPTH, 1, HG_WIDTH), "lam": lam.reshape(DEPTH, 1),
    }
    return (_trunk(x_prompt, p), _trunk(x_sample, p))
```

```python
import functools
import math

import jax
import jax.numpy as jnp
from jax import lax
from jax.experimental import pallas as pl
from jax.experimental.pallas import tpu as pltpu

D_MODEL = 1024
DEPTH = 2
ATTN_WIDTH = 512
HG_WIDTH = 512
HEAD_W = 128
N_HEADS = 4
ATTN_HEAD_DIM = 64
ROT_DIM = 16
ROPE_THETA = 500000.0
D_FF = 2816
ALPHA = (2 * DEPTH) ** 0.25
EPS = 1e-5
IN_WIDTH = 3 * ATTN_WIDTH + 5 * HG_WIDTH
LOG2E = math.log2(math.e)
ATTN_Q_SCALE = ATTN_HEAD_DIM ** -0.5 * LOG2E

ROW_TILE = 512
FFN_ROW_TILE = 2 * ROW_TILE
FF_CHUNK = 512
ATTN_TQ = 1024
ATTN_UNROLL = 2
ATTN_TK = 2048
VT_ROWS = HEAD_W + 16
HG_BLOCK = 512
HG_UNROLL = 8
HG_STEP = 16
VMEM_LIMIT = 56 * 1024 * 1024

BF16 = jnp.bfloat16
F32 = jnp.float32
F8 = jnp.float8_e4m3fn
F8_BLOCK_MAX = 128.0
F32_EXPONENT_MASK = 0x7F800000
F8_MIN_BLOCK_MAX = 1e-30


def _layer_norm(y, g, b):
    mu = jnp.mean(y, axis=-1, keepdims=True)
    yc = y - mu
    var = jnp.mean(yc * yc, axis=-1, keepdims=True)
    return yc * lax.rsqrt(var + EPS) * g + b


def _sigmoid(z):
    return 1.0 / (1.0 + jnp.exp(-z))


def _ffn_block(x, wg_ref, wu_ref, wd_ref, g, b):
    xb = x.astype(BF16)
    acc = jnp.zeros((x.shape[0], D_MODEL), F32)
    for f0 in range(0, D_FF, FF_CHUNK):
        f1 = min(f0 + FF_CHUNK, D_FF)
        hg = jnp.dot(xb, wg_ref[:, f0:f1], preferred_element_type=F32)
        hu = jnp.dot(xb, wu_ref[:, f0:f1], preferred_element_type=F32)
        h = (hg * _sigmoid(hg) * hu).astype(BF16)
        acc = acc + jnp.dot(h, wd_ref[f0:f1, :], preferred_element_type=F32)
    return _layer_norm(ALPHA * x + 0.5 * acc, g, b)


def _ffn_kernel(x_ref, wg_ref, wu_ref, wd_ref, g_ref, b_ref, o_ref):
    for r in range(0, FFN_ROW_TILE, ROW_TILE):
        o_ref[r:r + ROW_TILE, :] = _ffn_block(x_ref[r:r + ROW_TILE, :], wg_ref, wu_ref, wd_ref,
                                              g_ref[...], b_ref[...])


def _stacked(shape, *lead):
    return pl.BlockSpec((None,) * len(lead) + shape, lambda i: lead + (0,) * len(shape),
                        pipeline_mode=pl.Buffered(1))


def _ffn(x, wg, wu, wd, g, b, l, j):
    t = x.shape[0]
    row = lambda i: (i, 0)
    const = lambda i: (0, 0)
    return pl.pallas_call(
        _ffn_kernel,
        grid=(t // FFN_ROW_TILE,),
        in_specs=[
            pl.BlockSpec((FFN_ROW_TILE, D_MODEL), row),
            _stacked((D_MODEL, D_FF), l, j),
            _stacked((D_MODEL, D_FF), l, j),
            _stacked((D_FF, D_MODEL), l, j),
            pl.BlockSpec((1, D_MODEL), const),
            pl.BlockSpec((1, D_MODEL), const),
        ],
        out_specs=pl.BlockSpec((FFN_ROW_TILE, D_MODEL), row),
        out_shape=jax.ShapeDtypeStruct((t, D_MODEL), F32),
        compiler_params=pltpu.CompilerParams(
            dimension_semantics=("arbitrary",), vmem_limit_bytes=VMEM_LIMIT),
        name="ffn",
    )(x, wg, wu, wd, g, b)


def _rope(x, c, a, bt):
    outs = []
    for h in range(N_HEADS):
        xh = x[:, h * HEAD_W:(h + 1) * HEAD_W]
        up = pltpu.roll(xh, HEAD_W - ROT_DIM // 2, axis=1)
        dn = pltpu.roll(xh, ROT_DIM // 2, axis=1)
        outs.append(xh * c + up * a + dn * bt)
    return outs


def _log_forget(z, lb):
    z2 = z * LOG2E
    log_sig = jnp.minimum(z2, 0.0) - jnp.log2(1.0 + jnp.exp2(-jnp.abs(z2)))
    a = jnp.log2(lb)
    c = jnp.log2(1.0 - lb) + log_sig
    log_f = jnp.maximum(a, c) + jnp.log2(1.0 + jnp.exp2(-jnp.abs(a - c)))
    return jnp.minimum(log_f, 0.0)


def _in_proj_kernel(x_ref, w_ref, wvt_ref, c_ref, a_ref, bt_ref, lbf_ref, lbb_ref,
                    aq_ref, ak_ref, avt_ref, hq_ref, gf_ref, gb_ref, hi_ref, gate_ref):
    xb = x_ref[...].astype(BF16)
    c, a, bt = c_ref[...], a_ref[...], bt_ref[...]

    def proj(j):
        return jnp.dot(xb, w_ref[:, j * 512:(j + 1) * 512], preferred_element_type=F32)

    gf_ref[...] = _log_forget(proj(4), lbf_ref[...])
    gb_ref[...] = _log_forget(proj(5), lbb_ref[...])
    q = _rope(proj(0), c, a, bt)
    for h in range(N_HEADS):
        aq_ref[:, h * HEAD_W:(h + 1) * HEAD_W] = (q[h] * ATTN_Q_SCALE).astype(BF16)
    k = _rope(proj(1), c, a, bt)
    for h in range(N_HEADS):
        ak_ref[:, h * HEAD_W:(h + 1) * HEAD_W] = k[h].astype(BF16)
    hq = proj(3)
    hq_ref[...] = hq * _sigmoid(hq)
    gt = proj(7)
    gate_ref[...] = gt * _sigmoid(gt)
    vt = lax.dot_general(wvt_ref[...], xb, (((1,), (1,)), ((), ())), preferred_element_type=F32).astype(BF16)
    ones = jnp.ones((VT_ROWS - HEAD_W, vt.shape[1]), BF16)
    for h in range(N_HEADS):
        avt_ref[h * VT_ROWS:h * VT_ROWS + HEAD_W, :] = vt[h * HEAD_W:(h + 1) * HEAD_W, :]
        avt_ref[h * VT_ROWS + HEAD_W:(h + 1) * VT_ROWS, :] = ones
    hi_ref[...] = proj(6).astype(BF16)


def _in_proj(x, w_in, wvt, rope_c, rope_a, rope_b, lb_f, lb_b, seq, l):
    t = x.shape[0]
    nseq = seq // ROW_TILE
    row = lambda i: (i, 0)
    const = lambda i: (0, 0)
    pos = lambda i: (i % nseq, 0)
    grp = pl.BlockSpec((ROW_TILE, 512), row)
    shp = lambda dt: jax.ShapeDtypeStruct((t, 512), dt)
    return pl.pallas_call(
        _in_proj_kernel,
        grid=(t // ROW_TILE,),
        in_specs=[
            pl.BlockSpec((ROW_TILE, D_MODEL), row),
            _stacked((D_MODEL, IN_WIDTH), l),
            _stacked((ATTN_WIDTH, D_MODEL), l),
            pl.BlockSpec((ROW_TILE, HEAD_W), pos),
            pl.BlockSpec((ROW_TILE, HEAD_W), pos),
            pl.BlockSpec((ROW_TILE, HEAD_W), pos),
            pl.BlockSpec((1, HG_WIDTH), const),
            pl.BlockSpec((1, HG_WIDTH), const),
        ],
        out_specs=[grp, grp, pl.BlockSpec((N_HEADS * VT_ROWS, ROW_TILE), lambda i: (0, i)),
                   grp, grp, grp, grp, grp],
        out_shape=[shp(BF16), shp(BF16), jax.ShapeDtypeStruct((N_HEADS * VT_ROWS, t), BF16),
                   shp(F32), shp(F32), shp(F32), shp(BF16), shp(F32)],
        compiler_params=pltpu.CompilerParams(
            dimension_semantics=("arbitrary",), vmem_limit_bytes=VMEM_LIMIT),
        name="in_proj",
    )(x, w_in, wvt, rope_c, rope_a, rope_b, lb_f, lb_b)


def _component_max(a):
    first = lax.broadcasted_iota(jnp.int32, a.shape, 1) < ATTN_HEAD_DIM
    m0 = jnp.max(jnp.where(first, a, 0.0), axis=1, keepdims=True)
    m1 = jnp.max(jnp.where(first, 0.0, a), axis=1, keepdims=True)
    return jnp.maximum(jnp.where(first, m0, m1), F8_MIN_BLOCK_MAX)


def _f8_scale(block_max):
    bits = lax.bitcast_convert_type(F8_BLOCK_MAX / block_max, jnp.int32)
    return lax.bitcast_convert_type(bits & F32_EXPONENT_MASK, F32)


def _f8_features(x, keys):
    hi = x.astype(F8).astype(F32)
    lo = x - hi
    first = lax.broadcasted_iota(jnp.int32, x.shape, 1) < ATTN_HEAD_DIM
    hi_sw = pltpu.roll(hi, ATTN_HEAD_DIM, axis=1)
    lo_sw = pltpu.roll(lo, ATTN_HEAD_DIM, axis=1)
    zero = jnp.zeros_like(x)
    if keys:
        groups = ((jnp.where(first, hi, lo_sw), jnp.where(first, hi, zero)),
                  (jnp.where(first, hi_sw, lo), jnp.where(first, hi_sw, zero)))
    else:
        groups = ((jnp.where(first, hi, hi_sw), jnp.where(first, lo, zero)),
                  (jnp.where(first, hi_sw, hi), jnp.where(first, lo_sw, zero)))
    return [jnp.concatenate(g, axis=1).astype(F8) for g in groups]


def _attn_kernel(lam_ref, q_ref, k_ref, vt_ref, g_ref, o_ref, kf_ref, sa_ref, sb_ref, m_ref, acc_ref, *,
                 seq, out_scale):
    n_chunks = seq // ATTN_TK

    def chunk(j):
        return pl.ds(pl.multiple_of(j * ATTN_TK, ATTN_TK), ATTN_TK)

    def kmax_body(j, a):
        return jnp.maximum(a, jnp.max(jnp.abs(k_ref[chunk(j), :].astype(F32)), axis=0, keepdims=True))

    kscale = _f8_scale(_component_max(lax.fori_loop(0, n_chunks, kmax_body, jnp.zeros((1, HEAD_W), F32))))

    def kfeat_body(j, carry):
        feats = _f8_features(k_ref[chunk(j), :].astype(F32) * kscale, keys=True)
        for c in range(2):
            kf_ref[c, chunk(j), :] = feats[c]
        return carry

    lax.fori_loop(0, n_chunks, kfeat_body, 0)

    def qblock(i, carry):
        rows = pl.ds(pl.multiple_of(i * ATTN_TQ, ATTN_TQ), ATTN_TQ)
        _attn_qblock(lam_ref, q_ref[rows, :], kscale, kf_ref, vt_ref, g_ref, o_ref.at[rows, :],
                     sa_ref, sb_ref, m_ref, acc_ref, seq=seq, out_scale=out_scale)
        return carry

    lax.fori_loop(0, seq // ATTN_TQ, qblock, 0)


def _attn_qblock(lam_ref, q, kscale, kf_ref, vt_ref, g_ref, o_ref, sa_ref, sb_ref, m_ref, acc_ref, *, seq, out_scale):
    qf32 = q.astype(F32)
    qscale = _f8_scale(_component_max(jnp.max(jnp.abs(qf32), axis=0, keepdims=True)))
    qf = _f8_features(qf32 * qscale, keys=False)
    unscale = ((1.0 / kscale) * (1.0 / qscale)).astype(BF16)
    unscale = (unscale[:, 0:1], unscale[:, HEAD_W - 1:HEAD_W])
    nt = (((1,), (1,)), ((), ()))
    n_chunks = seq // ATTN_TK

    def chunk(j):
        return pl.ds(pl.multiple_of(j * ATTN_TK, ATTN_TK), ATTN_TK)

    def scores(j, s_ref):
        for c in range(2):
            s = lax.dot_general(kf_ref[c, chunk(j), :], qf[c], nt, preferred_element_type=F32)
            s_ref[c] = s.astype(BF16) * unscale[c]

    def accumulate(j, s_ref):
        vt = vt_ref[:, chunk(j)]
        for c in range(2):
            s = s_ref[c]
            m = m_ref[c]
            m_new = jnp.maximum(m, jnp.max(s, axis=0, keepdims=True).astype(F32))
            e = jnp.exp2(s - m_new.astype(BF16))
            m_ref[c] = m_new
            acc_ref[c] = jnp.exp2(m - m_new) * acc_ref[c] + jnp.dot(vt, e, preferred_element_type=F32)

    m_ref[...] = jnp.full(m_ref.shape, -jnp.inf, F32)
    acc_ref[...] = jnp.zeros(acc_ref.shape, F32)
    s_bufs = (sa_ref, sb_ref)
    scores(0, s_bufs[0])

    unroll = min(ATTN_UNROLL, n_chunks)

    def body(i, carry):
        j0 = unroll * i
        for u in range(unroll):
            scores(j0 + u + 1, s_bufs[(u + 1) % 2])
            accumulate(j0 + u, s_bufs[u % 2])
        return carry

    lax.fori_loop(0, n_chunks // unroll - 1, body, 0)
    j0 = n_chunks - unroll
    for u in range(unroll):
        if u + 1 < unroll:
            scores(j0 + u + 1, s_bufs[(u + 1) % 2])
        accumulate(j0 + u, s_bufs[u % 2])
    lam = lam_ref[0]
    a1, a2 = acc_ref[0], acc_ref[1]
    ot = a1[:HEAD_W, :] / a1[HEAD_W:HEAD_W + 1, :] - lam * (a2[:HEAD_W, :] / a2[HEAD_W:HEAD_W + 1, :])
    ms = jnp.mean(ot * ot, axis=0, keepdims=True)
    ot = ot * lax.rsqrt(ms + EPS) * (g_ref[...] * out_scale)
    o_ref[...] = ot.T.astype(BF16)


def _attn(lam, aq, ak, avt, g, batch, seq, out_scale):
    t = aq.shape[0]
    return pl.pallas_call(
        functools.partial(_attn_kernel, seq=seq, out_scale=out_scale),
        grid=(batch, N_HEADS),
        in_specs=[
            pl.BlockSpec(memory_space=pltpu.SMEM),
            pl.BlockSpec((seq, HEAD_W), lambda b, h: (b, h)),
            pl.BlockSpec((seq, HEAD_W), lambda b, h: (b, h)),
            pl.BlockSpec((VT_ROWS, seq), lambda b, h: (h, b)),
            pl.BlockSpec((HEAD_W, 1), lambda b, h: (0, 0)),
        ],
        out_specs=pl.BlockSpec((seq, HEAD_W), lambda b, h: (b, h)),
        out_shape=jax.ShapeDtypeStruct((t, ATTN_WIDTH), BF16),
        scratch_shapes=[
            pltpu.VMEM((2, seq, 2 * HEAD_W), F8),
            pltpu.VMEM((2, ATTN_TK, ATTN_TQ), BF16),
            pltpu.VMEM((2, ATTN_TK, ATTN_TQ), BF16),
            pltpu.VMEM((2, 1, ATTN_TQ), F32),
            pltpu.VMEM((2, VT_ROWS, ATTN_TQ), F32),
        ],
        compiler_params=pltpu.CompilerParams(
            dimension_semantics=("arbitrary", "arbitrary"), vmem_limit_bytes=VMEM_LIMIT),
        name="attn",
    )(lam, aq, ak, avt, g)


def _cumsum16(g, reverse):
    row = lax.broadcasted_iota(jnp.int32, g.shape, 0)
    b = g
    for sh in (1, 2, 4, 8):
        if reverse:
            b = b + jnp.where(row < HG_STEP - sh, pltpu.roll(b, HG_STEP - sh, axis=0), 0.0)
        else:
            b = b + jnp.where(row >= sh, pltpu.roll(b, sh, axis=0), 0.0)
    return b


def _hgrn_state(q, kk, b, v, st, reverse):
    qd = (q * jnp.exp2(b)).astype(BF16)
    o = lax.dot_general(qd, st.astype(BF16), (((1,), (1,)), ((), ())), preferred_element_type=F32)
    b_end = b[0:1, :] if reverse else b[HG_STEP - 1:HG_STEP, :]
    kd = (kk * jnp.exp2(b_end - b)).astype(BF16)
    upd = lax.dot_general(v, kd, (((0,), (0,)), ((), ())), preferred_element_type=F32)
    return o, st * jnp.exp2(b_end) + upd


def _hgrn_pairwise(q, kk, b, v, reverse):
    c = b - jnp.log2(kk)
    half = HG_STEP // 2
    row = lax.broadcasted_iota(jnp.int32, (half, HEAD_W), 0)
    vf = v.astype(F32)
    o = [jnp.zeros((half, HEAD_W), F32), jnp.zeros((half, HEAD_W), F32)]
    for s in range(HG_STEP):
        cs = c[s:s + 1, :]
        vs = vf[s:s + 1, :]
        for hi in range(2):
            r0 = hi * half
            if (r0 > s) if reverse else (r0 + half - 1 < s):
                continue
            d = b[r0:r0 + half, :] - cs
            if (r0 + half - 1 > s) if reverse else (r0 < s):
                seen = (row + r0 <= s) if reverse else (row + r0 >= s)
                d = jnp.where(seen, d, -jnp.inf)
            col = jnp.sum(q[r0:r0 + half, :] * jnp.exp2(d), axis=1, keepdims=True)
            o[hi] = o[hi] + col * vs
    return jnp.concatenate(o, axis=0)


def _hgrn_kernel(qf_ref, gf_ref, vf_ref, qb_ref, gb_ref, vb_ref, of_ref, ob_ref, sf_ref, sb_ref):
    @pl.when(pl.program_id(1) == 0)
    def _():
        sf_ref[...] = jnp.zeros_like(sf_ref)
        sb_ref[...] = jnp.zeros_like(sb_ref)

    nstep = HG_BLOCK // HG_STEP

    def body(trip, carry):
        for sub in range(HG_UNROLL):
            one_step(HG_UNROLL * trip + sub)
        return carry

    def one_step(i):
        rf = pl.multiple_of(i * HG_STEP, HG_STEP)
        rb = pl.multiple_of((nstep - 1 - i) * HG_STEP, HG_STEP)
        chains = []
        for h in range(N_HEADS):
            cols = slice(h * HEAD_W, (h + 1) * HEAD_W)
            chains.append((qf_ref, gf_ref, vf_ref, of_ref, sf_ref, rf, h, cols, False))
            chains.append((qb_ref, gb_ref, vb_ref, ob_ref, sb_ref, rb, h, cols, True))
        live = []
        for q_ref, g_ref, v_ref, o_ref, s_ref, r, h, cols, reverse in chains:
            rows = pl.ds(r, HG_STEP)
            q, g, v = q_ref[rows, cols], g_ref[rows, cols], v_ref[rows, cols]
            kk = 1.0 - jnp.exp2(g)
            b = _cumsum16(g, reverse)
            o, st = _hgrn_state(q, kk, b, v, s_ref[h], reverse)
            o_ref[rows, cols] = o
            s_ref[h] = st
            live.append((q, kk, b, v))
        for (q_ref, g_ref, v_ref, o_ref, s_ref, r, h, cols, reverse), (q, kk, b, v) in zip(chains, live):
            rows = pl.ds(r, HG_STEP)
            o_ref[rows, cols] += _hgrn_pairwise(q, kk, b, v, reverse)

    lax.fori_loop(0, nstep // HG_UNROLL, body, 0)


def _hgrn(hq, gf, gb, hi, batch, seq):
    t = hq.shape[0]
    nb = seq // HG_BLOCK
    fwd = pl.BlockSpec((HG_BLOCK, HG_WIDTH), lambda b, j: (b * nb + j, 0))
    bwd = pl.BlockSpec((HG_BLOCK, HG_WIDTH), lambda b, j: (b * nb + nb - 1 - j, 0))
    return pl.pallas_call(
        _hgrn_kernel,
        grid=(batch, nb),
        in_specs=[fwd, fwd, fwd, bwd, bwd, bwd],
        out_specs=[fwd, bwd],
        out_shape=[jax.ShapeDtypeStruct((t, HG_WIDTH), F32)] * 2,
        scratch_shapes=[pltpu.VMEM((N_HEADS, HEAD_W, HEAD_W), F32)] * 2,
        compiler_params=pltpu.CompilerParams(
            dimension_semantics=("arbitrary", "arbitrary"), vmem_limit_bytes=VMEM_LIMIT),
        name="hgrn",
    )(hq, gf, hi, hq, gb, hi)


def _out_proj_ffn_kernel(x_ref, ao_ref, of_ref, ob_ref, gate_ref, hgg_ref, w_ref, g1_ref, b1_ref,
                         wg_ref, wu_ref, wd_ref, g2_ref, b2_ref, o_ref):
    x = x_ref[...]
    ho = of_ref[...] + ob_ref[...]
    gate = gate_ref[...]
    hgg = hgg_ref[...]
    mix = jnp.dot(ao_ref[...], w_ref[0:ATTN_WIDTH, :], preferred_element_type=F32)
    for h in range(N_HEADS):
        cols = slice(h * HEAD_W, (h + 1) * HEAD_W)
        oh = ho[:, cols]
        ms = jnp.mean(oh * oh, axis=-1, keepdims=True)
        nh = (oh * lax.rsqrt(ms + EPS) * hgg * gate[:, cols]).astype(BF16)
        mix = mix + jnp.dot(nh, w_ref[ATTN_WIDTH + h * HEAD_W:ATTN_WIDTH + (h + 1) * HEAD_W, :],
                            preferred_element_type=F32)
    x = _layer_norm(ALPHA * x + mix, g1_ref[...], b1_ref[...])
    o_ref[...] = _ffn_block(x, wg_ref, wu_ref, wd_ref, g2_ref[...], b2_ref[...])


def _out_proj_ffn(x, ao, of, ob, gate, hgg, w_out, g1, b1, wg, wu, wd, g2, b2, l):
    t = x.shape[0]
    row = lambda i: (i, 0)
    const = lambda i: (0, 0)
    grp = pl.BlockSpec((ROW_TILE, 512), row)
    vec = pl.BlockSpec((1, D_MODEL), const)
    return pl.pallas_call(
        _out_proj_ffn_kernel,
        grid=(t // ROW_TILE,),
        in_specs=[
            pl.BlockSpec((ROW_TILE, D_MODEL), row),
            grp, grp, grp, grp,
            pl.BlockSpec((1, HEAD_W), const),
            _stacked((D_MODEL, D_MODEL), l),
            vec, vec,
            _stacked((D_MODEL, D_FF), l, 1),
            _stacked((D_MODEL, D_FF), l, 1),
            _stacked((D_FF, D_MODEL), l, 1),
            vec, vec,
        ],
        out_specs=pl.BlockSpec((ROW_TILE, D_MODEL), row),
        out_shape=jax.ShapeDtypeStruct((t, D_MODEL), F32),
        compiler_params=pltpu.CompilerParams(
            dimension_semantics=("arbitrary",), vmem_limit_bytes=VMEM_LIMIT),
        name="out_proj_ffn",
    )(x, ao, of, ob, gate, hgg, w_out, g1, b1, wg, wu, wd, g2, b2)


def _rope_tables(seq):
    half = ROT_DIM // 2
    lane = jnp.arange(HEAD_W) % ATTN_HEAD_DIM
    inv_freq = jnp.float32(ROPE_THETA) ** (-(lane % half).astype(F32) * 2.0 / ROT_DIM)
    ang = jnp.arange(seq, dtype=F32)[:, None] * inv_freq[None, :]
    cos, sin = jnp.cos(ang), jnp.sin(ang)
    rope_c = jnp.where(lane < ROT_DIM, cos, 1.0)
    rope_a = jnp.where(lane < half, -sin, 0.0)
    rope_b = jnp.where((lane >= half) & (lane < ROT_DIM), sin, 0.0)
    return rope_c, rope_a, rope_b


def _trunk(x3, p):
    batch, seq, _ = x3.shape
    x = x3.reshape(batch * seq, D_MODEL)
    rope_c, rope_a, rope_b = _rope_tables(seq)
    for l in range(DEPTH):
        lam_init = 0.8 - 0.6 * math.exp(-0.3 * l)
        x = _ffn(x, p["wg"], p["wu"], p["wd"], p["ln_g"][l][0], p["ln_b"][l][0], l, 0)
        aq, ak, avt, hq, gf, gb, hi, gate = _in_proj(
            x, p["w_in"], p["wvt"], rope_c, rope_a, rope_b, p["lbs"][0][l], p["lbs"][1][l], seq, l)
        ao = _attn(p["lam"][l], aq, ak, avt, p["attn_g"][l], batch, seq, 1.0 - lam_init)
        of, ob = _hgrn(hq, gf, gb, hi, batch, seq)
        x = _out_proj_ffn(x, ao, of, ob, gate, p["hg_g"][l], p["w_out"], p["ln_g"][l][1], p["ln_b"][l][1],
                          p["wg"], p["wu"], p["wd"], p["ln_g"][l][2], p["ln_b"][l][2], l)
    return x.reshape(batch, seq, D_MODEL)


def kernel(x_prompt, x_sample, w_in, w_out, attn_lambda, attn_norm_g, hg_norm_g, hg_lower_bound,
           ffn_w_gate, ffn_w_up, ffn_w_down, ln_g, ln_b):
    sm = jax.nn.softmax(hg_lower_bound.astype(F32), axis=1)
    lbs = jnp.maximum(jnp.cumsum(sm, axis=1) - sm[:, :1], 0.0)
    lp = attn_lambda.astype(F32)
    lam_init = jnp.asarray([0.8 - 0.6 * math.exp(-0.3 * l) for l in range(DEPTH)], F32)
    lam = (jnp.exp(jnp.sum(lp[:, 0] * lp[:, 1], axis=-1))
           - jnp.exp(jnp.sum(lp[:, 2] * lp[:, 3], axis=-1)) + lam_init)
    p = {
        "w_in": w_in.astype(BF16), "w_out": w_out.astype(BF16),
        "wvt": jnp.swapaxes(w_in[:, :, 2 * ATTN_WIDTH:3 * ATTN_WIDTH], 1, 2).astype(BF16),
        "wg": ffn_w_gate.astype(BF16), "wu": ffn_w_up.astype(BF16), "wd": ffn_w_down.astype(BF16),
        "ln_g": ln_g.reshape(DEPTH, 3, 1, D_MODEL), "ln_b": ln_b.reshape(DEPTH, 3, 1, D_MODEL),
        "attn_g": attn_norm_g.reshape(DEPTH, HEAD_W, 1), "hg_g": hg_norm_g.reshape(DEPTH, 1, HEAD_W),
        "lbs": lbs.reshape(2, DEPTH, 1, HG_WIDTH), "lam": lam.reshape(DEPTH, 1),
    }
    return (_trunk(x_prompt, p), _trunk(x_sample, p))
```
